```python
import math
import jax, jax.numpy as jnp
from jax import lax
import numpy as np

D_MODEL = 1024
BATCH = 8
SEQ = 16384
DEPTH = 2

N_META = 16
MIX_WIDTH = D_MODEL
HEAD_DIM = 64
SB_WIDTH = MIX_WIDTH // 4
SB_HEADS = SB_WIDTH // HEAD_DIM
SB_BLOCK = 128
DN_WIDTH = MIX_WIDTH // 4
DN_HEADS = DN_WIDTH // HEAD_DIM
DN_CONV = 4
DN_CHUNK = 64
S5_WIDTH = MIX_WIDTH - SB_WIDTH - DN_WIDTH
S5_GROUP = 16
S5_GROUPS = S5_WIDTH // S5_GROUP
S5_STATE = 64
D_FF = ((8 * D_MODEL // 3 + 127) // 128) * 128
IN_SPLITS = (SB_WIDTH, SB_WIDTH, SB_WIDTH, 3 * DN_WIDTH, DN_WIDTH, DN_HEADS, DN_HEADS, S5_WIDTH)
IN_WIDTH = sum(IN_SPLITS)
EPS = 1e-6

kernel_name = 'hymba_sb_deltanet_s5_macaron'


def rmsnorm(x, g):
    xf = x.astype(jnp.float32)
    y = xf * lax.rsqrt(jnp.mean(xf * xf, axis=-1, keepdims=True) + EPS)
    return (y * g.astype(jnp.float32)).astype(x.dtype)


def l2norm(x):
    xf = x.astype(jnp.float32)
    return xf * lax.rsqrt(jnp.sum(xf * xf, axis=-1, keepdims=True) + EPS)


def swiglu(x, w_gate, w_up, w_down):
    return (jax.nn.silu(x @ w_gate) * (x @ w_up)) @ w_down


def front_pad(x, n):
    return jnp.pad(x, [(0, 0), (n, 0)] + [(0, 0)] * (x.ndim - 2))


def causal_depthwise_conv(x, w):
    k = w.shape[0]
    return lax.conv_general_dilated(
        x, w[:, None, :], window_strides=(1,), padding=[(k - 1, 0)],
        dimension_numbers=('NWC', 'WIO', 'NWC'), feature_group_count=x.shape[-1])


def stick_breaking_attention(q, k, v):
    bsz, length, nh, hd = q.shape
    pad = (-N_META) % SB_BLOCK
    q, k, v = (jnp.transpose(front_pad(t, pad), (0, 2, 1, 3)) for t in (q, k, v))
    lp = length + pad
    n_blocks = lp // SB_BLOCK
    key_pos = jnp.arange(lp)
    scale = hd ** -0.5

    def block(i):
        q_blk = lax.dynamic_slice_in_dim(q, i * SB_BLOCK, SB_BLOCK, axis=2)
        q_pos = i * SB_BLOCK + jnp.arange(SB_BLOCK)
        z = jnp.einsum('bhqd,bhkd->bhqk', q_blk, k, preferred_element_type=jnp.float32) * scale
        valid = (key_pos[None, :] < q_pos[:, None]) & (key_pos[None, :] >= pad)
        log_keep = jnp.where(valid, jax.nn.log_sigmoid(-z), 0.0)
        after = lax.cumsum(log_keep, axis=3, reverse=True) - log_keep
        w = jnp.where(valid, jnp.exp(jax.nn.log_sigmoid(z) + after), 0.0)
        return jnp.einsum('bhqk,bhkd->bhqd', w.astype(v.dtype), v)

    out = lax.map(block, jnp.arange(n_blocks))
    out = jnp.transpose(out, (1, 0, 3, 2, 4)).reshape(bsz, lp, nh, hd)
    return out[:, pad:]


def chunk_gated_delta_rule(q, k, v, g, beta):
    bsz, nh, length, dk = q.shape
    dv = v.shape[-1]
    c = DN_CHUNK
    n = length // c
    q = q * dk ** -0.5
    qc = q.reshape(bsz, nh, n, c, dk)
    kc = k.reshape(bsz, nh, n, c, dk)
    vc = v.reshape(bsz, nh, n, c, dv)
    bc = beta.reshape(bsz, nh, n, c, 1)
    gc = jnp.cumsum(g.reshape(bsz, nh, n, c), axis=-1)
    incl = jnp.tril(jnp.ones((c, c), dtype=bool))
    strict = jnp.tril(jnp.ones((c, c), dtype=bool), -1)
    decay = jnp.exp(jnp.where(incl, gc[..., :, None] - gc[..., None, :], -jnp.inf))
    kb = kc * bc
    lmat = jnp.where(strict, jnp.einsum('bhnid,bhnjd->bhnij', kb, kc) * decay, 0.0)
    eye = jnp.eye(c, dtype=jnp.float32)
    t_inv = lax.linalg.triangular_solve(lmat + eye, jnp.broadcast_to(eye, lmat.shape),
                                        left_side=True, lower=True, unit_diagonal=True)
    u = jnp.einsum('bhnij,bhnjd->bhnid', t_inv, vc * bc)
    w = jnp.einsum('bhnij,bhnjd->bhnid', t_inv, kb * jnp.exp(gc)[..., None])
    attn = jnp.where(incl, jnp.einsum('bhnid,bhnjd->bhnij', qc, kc) * decay, 0.0)

    def step(state, inp):
        q_i, k_i, u_i, w_i, a_i, g_i = inp
        v_new = u_i - jnp.einsum('bhcd,bhde->bhce', w_i, state)
        o_i = (jnp.einsum('bhcd,bhde->bhce', q_i * jnp.exp(g_i)[..., None], state)
               + jnp.einsum('bhij,bhje->bhie', a_i, v_new))
        g_last = g_i[..., -1:]
        state = (state * jnp.exp(g_last)[..., None]
                 + jnp.einsum('bhcd,bhce->bhde', k_i * jnp.exp(g_last - g_i)[..., None], v_new))
        return state, o_i

    chunks = tuple(jnp.moveaxis(t, 2, 0) for t in (qc, kc, u, w, attn, gc))
    state0 = jnp.zeros((bsz, nh, dk, dv), jnp.float32)
    _, out = lax.scan(step, state0, chunks)
    return jnp.moveaxis(out, 0, 2).reshape(bsz, nh, length, dv)


def gated_deltanet(qkv, z, b, a, conv_w, a_log, dt_bias, out_norm):
    bsz, length, _ = qkv.shape
    qkv = jax.nn.silu(causal_depthwise_conv(qkv, conv_w))
    q, k, v = (t.reshape(bsz, length, DN_HEADS, HEAD_DIM) for t in jnp.split(qkv, 3, axis=-1))
    q, k, v = l2norm(q), l2norm(k), v.astype(jnp.float32)
    beta = jax.nn.sigmoid(b.astype(jnp.float32))
    g = -jnp.exp(a_log.astype(jnp.float32)) * jax.nn.softplus(a.astype(jnp.float32) + dt_bias.astype(jnp.float32))
    pad = (-N_META) % DN_CHUNK
    q, k, v, g, beta = (front_pad(t, pad) for t in (q, k, v, g, beta))
    o = chunk_gated_delta_rule(jnp.moveaxis(q, 2, 1), jnp.moveaxis(k, 2, 1), jnp.moveaxis(v, 2, 1),
                               jnp.moveaxis(g, 2, 1), jnp.moveaxis(beta, 2, 1))
    o = jnp.moveaxis(o, 1, 2)[:, pad:]
    o = rmsnorm(o, out_norm) * jax.nn.silu(z.astype(jnp.float32).reshape(bsz, length, DN_HEADS, HEAD_DIM))
    return o.reshape(bsz, length, DN_WIDTH)


def s5_mixer(u, a_re, a_im, log_dt, b_re, b_im, c_re, c_im, d, w_glu, b_glu):
    bsz, length, _ = u.shape
    f32 = jnp.float32
    uf = u.astype(f32).reshape(bsz, length, S5_GROUPS, S5_GROUP)
    lam = lax.complex(a_re.astype(f32), a_im.astype(f32))
    dt = jnp.exp(log_dt.astype(f32))[:, None]
    log_abar = lam * dt
    abar = jnp.exp(log_abar)
    b_bar = ((abar - 1.0) / lam)[..., None] * lax.complex(b_re.astype(f32), b_im.astype(f32))
    bu = jnp.einsum('blgc,gpc->blgp', uf.astype(jnp.complex64), b_bar)
    steps = jnp.ones((1, length, 1, 1), f32)

    def combine(e1, e2):
        n1, x1 = e1
        n2, x2 = e2
        return n1 + n2, x1 * jnp.exp(n2 * log_abar) + x2

    _, states = lax.associative_scan(combine, (steps, bu), axis=1)
    c_cplx = lax.complex(c_re.astype(f32), c_im.astype(f32))
    y = jnp.real(jnp.einsum('blgp,gcp->blgc', states, c_cplx)) + d.astype(f32).reshape(S5_GROUPS, S5_GROUP) * uf
    y = jax.nn.gelu(y.reshape(bsz, length, S5_WIDTH))
    return y * jax.nn.sigmoid(y @ w_glu.astype(f32) + b_glu.astype(f32))


def hybrid_mixer(h, w_in, sb_out_norm, dn_conv_w, dn_a_log, dn_dt_bias, dn_out_norm,
                 s5_a_re, s5_a_im, s5_log_dt, s5_b_re, s5_b_im, s5_c_re, s5_c_im,
                 s5_d, s5_w_glu, s5_b_glu, s5_out_norm, w_out):
    bsz, length, _ = h.shape
    proj = h @ w_in
    sb_q, sb_k, sb_v, dn_qkv, dn_z, dn_b, dn_a, s5_u = jnp.split(
        proj, np.cumsum(IN_SPLITS)[:-1].tolist(), axis=-1)
    heads = lambda t: t.reshape(bsz, length, SB_HEADS, HEAD_DIM)
    o_sb = stick_breaking_attention(heads(sb_q), heads(sb_k), heads(sb_v))
    o_sb = rmsnorm(o_sb, sb_out_norm).reshape(bsz, length, SB_WIDTH)
    o_dn = gated_deltanet(dn_qkv, dn_z, dn_b, dn_a, dn_conv_w, dn_a_log, dn_dt_bias, dn_out_norm)
    o_s5 = rmsnorm(s5_mixer(s5_u, s5_a_re, s5_a_im, s5_log_dt, s5_b_re, s5_b_im, s5_c_re, s5_c_im,
                            s5_d, s5_w_glu, s5_b_glu), s5_out_norm)
    mixed = jnp.concatenate([o_sb.astype(h.dtype), o_dn.astype(h.dtype), o_s5.astype(h.dtype)], axis=-1)
    return mixed @ w_out


def _fwd_setup_inputs(seed: int = 0) -> dict:
    key = jax.random.key(seed)
    ks = iter(jax.random.split(key, 40))
    f32 = jnp.float32
    nrm = lambda shape, scale: scale * jax.random.normal(next(ks), shape, f32)
    gain = lambda shape: 1.0 + nrm(shape, 0.02)
    unif = lambda shape, lo, hi: jax.random.uniform(next(ks), shape, f32, minval=lo, maxval=hi)
    dn_dt = jnp.exp(unif((DEPTH, DN_HEADS), math.log(1e-3), math.log(1e-1)))
    return {
        'x': nrm((BATCH, SEQ, D_MODEL), 1.0),
        'meta_tokens': nrm((N_META, D_MODEL), 1.0),
        'ffn1_norm': gain((DEPTH, D_MODEL)),
        'ffn1_w_gate': nrm((DEPTH, D_MODEL, D_FF), D_MODEL ** -0.5),
        'ffn1_w_up': nrm((DEPTH, D_MODEL, D_FF), D_MODEL ** -0.5),
        'ffn1_w_down': nrm((DEPTH, D_FF, D_MODEL), D_FF ** -0.5),
        'mix_norm': gain((DEPTH, D_MODEL)),
        'w_in': nrm((DEPTH, D_MODEL, IN_WIDTH), D_MODEL ** -0.5),
        'sb_out_norm': gain((DEPTH, HEAD_DIM)),
        'dn_conv_w': nrm((DEPTH, DN_CONV, 3 * DN_WIDTH), DN_CONV ** -0.5),
        'dn_a_log': jnp.log(unif((DEPTH, DN_HEADS), 1.0, 16.0)),
        'dn_dt_bias': dn_dt + jnp.log(-jnp.expm1(-dn_dt)),
        'dn_out_norm': gain((DEPTH, HEAD_DIM)),
        's5_a_re': -0.5 + nrm((DEPTH, S5_GROUPS, S5_STATE), 0.01),
        's5_a_im': math.pi * jnp.arange(S5_STATE, dtype=f32) + nrm((DEPTH, S5_GROUPS, S5_STATE), 0.01),
        's5_log_dt': unif((DEPTH, S5_GROUPS), math.log(1e-3), math.log(1e-1)),
        's5_b_re': nrm((DEPTH, S5_GROUPS, S5_STATE, S5_GROUP), (2 * S5_GROUP) ** -0.5),
        's5_b_im': nrm((DEPTH, S5_GROUPS, S5_STATE, S5_GROUP), (2 * S5_GROUP) ** -0.5),
        's5_c_re': nrm((DEPTH, S5_GROUPS, S5_GROUP, S5_STATE), (2 * S5_STATE) ** -0.5),
        's5_c_im': nrm((DEPTH, S5_GROUPS, S5_GROUP, S5_STATE), (2 * S5_STATE) ** -0.5),
        's5_d': nrm((DEPTH, S5_WIDTH), 1.0),
        's5_w_glu': nrm((DEPTH, S5_WIDTH, S5_WIDTH), S5_WIDTH ** -0.5),
        's5_b_glu': nrm((DEPTH, S5_WIDTH), 0.02),
        's5_out_norm': gain((DEPTH, S5_WIDTH)),
        'w_out': nrm((DEPTH, MIX_WIDTH, D_MODEL), MIX_WIDTH ** -0.5),
        'ffn2_norm': gain((DEPTH, D_MODEL)),
        'ffn2_w_gate': nrm((DEPTH, D_MODEL, D_FF), D_MODEL ** -0.5),
        'ffn2_w_up': nrm((DEPTH, D_MODEL, D_FF), D_MODEL ** -0.5),
        'ffn2_w_down': nrm((DEPTH, D_FF, D_MODEL), D_FF ** -0.5),
        'final_norm': gain((D_MODEL,)),
    }


def _fwd_reference(x, meta_tokens, ffn1_norm, ffn1_w_gate, ffn1_w_up, ffn1_w_down, mix_norm, w_in,
              sb_out_norm, dn_conv_w, dn_a_log, dn_dt_bias, dn_out_norm,
              s5_a_re, s5_a_im, s5_log_dt, s5_b_re, s5_b_im, s5_c_re, s5_c_im,
              s5_d, s5_w_glu, s5_b_glu, s5_out_norm, w_out,
              ffn2_norm, ffn2_w_gate, ffn2_w_up, ffn2_w_down, final_norm):
    bsz = x.shape[0]
    meta = jnp.broadcast_to(meta_tokens[None].astype(x.dtype), (bsz, N_META, D_MODEL))
    h = jnp.concatenate([meta, x], axis=1)
    for l in range(DEPTH):
        h = h + 0.5 * swiglu(rmsnorm(h, ffn1_norm[l]), ffn1_w_gate[l], ffn1_w_up[l], ffn1_w_down[l])
        h = h + hybrid_mixer(rmsnorm(h, mix_norm[l]), w_in[l], sb_out_norm[l], dn_conv_w[l],
                             dn_a_log[l], dn_dt_bias[l], dn_out_norm[l],
                             s5_a_re[l], s5_a_im[l], s5_log_dt[l], s5_b_re[l], s5_b_im[l],
                             s5_c_re[l], s5_c_im[l], s5_d[l], s5_w_glu[l], s5_b_glu[l],
                             s5_out_norm[l], w_out[l])
        h = h + 0.5 * swiglu(rmsnorm(h, ffn2_norm[l]), ffn2_w_gate[l], ffn2_w_up[l], ffn2_w_down[l])
    return rmsnorm(h, final_norm)[:, N_META:]


import jax as _jax
import jax.numpy as _jnp

TWIN_FORMAT = 'train_step'
FWD_PARAMS = ['x', 'meta_tokens', 'ffn1_norm', 'ffn1_w_gate', 'ffn1_w_up', 'ffn1_w_down', 'mix_norm', 'w_in', 'sb_out_norm', 'dn_conv_w', 'dn_a_log', 'dn_dt_bias', 'dn_out_norm', 's5_a_re', 's5_a_im', 's5_log_dt', 's5_b_re', 's5_b_im', 's5_c_re', 's5_c_im', 's5_d', 's5_w_glu', 's5_b_glu', 's5_out_norm', 'w_out', 'ffn2_norm', 'ffn2_w_gate', 'ffn2_w_up', 'ffn2_w_down', 'final_norm']
TWIN_WEIGHTS = ['meta_tokens', 'ffn1_norm', 'ffn1_w_gate', 'ffn1_w_up', 'ffn1_w_down', 'mix_norm', 'w_in', 'sb_out_norm', 'dn_conv_w', 'dn_a_log', 'dn_dt_bias', 'dn_out_norm', 's5_a_re', 's5_a_im', 's5_log_dt', 's5_b_re', 's5_b_im', 's5_c_re', 's5_c_im', 's5_d', 's5_w_glu', 's5_b_glu', 's5_out_norm', 'w_out', 'ffn2_norm', 'ffn2_w_gate', 'ffn2_w_up', 'ffn2_w_down', 'final_norm']
TWIN_DIFF_INPUT = 'x'
TWIN_INPUTS = ['x', 'meta_tokens', 'ffn1_norm', 'ffn1_w_gate', 'ffn1_w_up', 'ffn1_w_down', 'mix_norm', 'w_in', 'sb_out_norm', 'dn_conv_w', 'dn_a_log', 'dn_dt_bias', 'dn_out_norm', 's5_a_re', 's5_a_im', 's5_log_dt', 's5_b_re', 's5_b_im', 's5_c_re', 's5_c_im', 's5_d', 's5_w_glu', 's5_b_glu', 's5_out_norm', 'w_out', 'ffn2_norm', 'ffn2_w_gate', 'ffn2_w_up', 'ffn2_w_down', 'final_norm', 'loss_target', 'm_meta_tokens', 'm_ffn1_norm', 'm_ffn1_w_gate', 'm_ffn1_w_up', 'm_ffn1_w_down', 'm_mix_norm', 'm_w_in', 'm_sb_out_norm', 'm_dn_conv_w', 'm_dn_a_log', 'm_dn_dt_bias', 'm_dn_out_norm', 'm_s5_a_re', 'm_s5_a_im', 'm_s5_log_dt', 'm_s5_b_re', 'm_s5_b_im', 'm_s5_c_re', 'm_s5_c_im', 'm_s5_d', 'm_s5_w_glu', 'm_s5_b_glu', 'm_s5_out_norm', 'm_w_out', 'm_ffn2_norm', 'm_ffn2_w_gate', 'm_ffn2_w_up', 'm_ffn2_w_down', 'm_final_norm', 'v_meta_tokens', 'v_ffn1_norm', 'v_ffn1_w_gate', 'v_ffn1_w_up', 'v_ffn1_w_down', 'v_mix_norm', 'v_w_in', 'v_sb_out_norm', 'v_dn_conv_w', 'v_dn_a_log', 'v_dn_dt_bias', 'v_dn_out_norm', 'v_s5_a_re', 'v_s5_a_im', 'v_s5_log_dt', 'v_s5_b_re', 'v_s5_b_im', 'v_s5_c_re', 'v_s5_c_im', 'v_s5_d', 'v_s5_w_glu', 'v_s5_b_glu', 'v_s5_out_norm', 'v_w_out', 'v_ffn2_norm', 'v_ffn2_w_gate', 'v_ffn2_w_up', 'v_ffn2_w_down', 'v_final_norm']
TWIN_OUTPUTS = ['loss', 'grad_x', 'grad_meta_tokens', 'grad_ffn1_norm', 'grad_ffn1_w_gate', 'grad_ffn1_w_up', 'grad_ffn1_w_down', 'grad_mix_norm', 'grad_w_in', 'grad_sb_out_norm', 'grad_dn_conv_w', 'grad_dn_a_log', 'grad_dn_dt_bias', 'grad_dn_out_norm', 'grad_s5_a_re', 'grad_s5_a_im', 'grad_s5_log_dt', 'grad_s5_b_re', 'grad_s5_b_im', 'grad_s5_c_re', 'grad_s5_c_im', 'grad_s5_d', 'grad_s5_w_glu', 'grad_s5_b_glu', 'grad_s5_out_norm', 'grad_w_out', 'grad_ffn2_norm', 'grad_ffn2_w_gate', 'grad_ffn2_w_up', 'grad_ffn2_w_down', 'grad_final_norm', 'delta_meta_tokens', 'delta_ffn1_norm', 'delta_ffn1_w_gate', 'delta_ffn1_w_up', 'delta_ffn1_w_down', 'delta_mix_norm', 'delta_w_in', 'delta_sb_out_norm', 'delta_dn_conv_w', 'delta_dn_a_log', 'delta_dn_dt_bias', 'delta_dn_out_norm', 'delta_s5_a_re', 'delta_s5_a_im', 'delta_s5_log_dt', 'delta_s5_b_re', 'delta_s5_b_im', 'delta_s5_c_re', 'delta_s5_c_im', 'delta_s5_d', 'delta_s5_w_glu', 'delta_s5_b_glu', 'delta_s5_out_norm', 'delta_w_out', 'delta_ffn2_norm', 'delta_ffn2_w_gate', 'delta_ffn2_w_up', 'delta_ffn2_w_down', 'delta_final_norm', 'new_m_meta_tokens', 'new_m_ffn1_norm', 'new_m_ffn1_w_gate', 'new_m_ffn1_w_up', 'new_m_ffn1_w_down', 'new_m_mix_norm', 'new_m_w_in', 'new_m_sb_out_norm', 'new_m_dn_conv_w', 'new_m_dn_a_log', 'new_m_dn_dt_bias', 'new_m_dn_out_norm', 'new_m_s5_a_re', 'new_m_s5_a_im', 'new_m_s5_log_dt', 'new_m_s5_b_re', 'new_m_s5_b_im', 'new_m_s5_c_re', 'new_m_s5_c_im', 'new_m_s5_d', 'new_m_s5_w_glu', 'new_m_s5_b_glu', 'new_m_s5_out_norm', 'new_m_w_out', 'new_m_ffn2_norm', 'new_m_ffn2_w_gate', 'new_m_ffn2_w_up', 'new_m_ffn2_w_down', 'new_m_final_norm', 'new_v_meta_tokens', 'new_v_ffn1_norm', 'new_v_ffn1_w_gate', 'new_v_ffn1_w_up', 'new_v_ffn1_w_down', 'new_v_mix_norm', 'new_v_w_in', 'new_v_sb_out_norm', 'new_v_dn_conv_w', 'new_v_dn_a_log', 'new_v_dn_dt_bias', 'new_v_dn_out_norm', 'new_v_s5_a_re', 'new_v_s5_a_im', 'new_v_s5_log_dt', 'new_v_s5_b_re', 'new_v_s5_b_im', 'new_v_s5_c_re', 'new_v_s5_c_im', 'new_v_s5_d', 'new_v_s5_w_glu', 'new_v_s5_b_glu', 'new_v_s5_out_norm', 'new_v_w_out', 'new_v_ffn2_norm', 'new_v_ffn2_w_gate', 'new_v_ffn2_w_up', 'new_v_ffn2_w_down', 'new_v_final_norm']
TWIN_LEAF_KINDS = {'loss': 'loss', 'grad_x': 'grad_x', 'grad_meta_tokens': 'grad_w', 'grad_ffn1_norm': 'grad_w', 'grad_ffn1_w_gate': 'grad_w', 'grad_ffn1_w_up': 'grad_w', 'grad_ffn1_w_down': 'grad_w', 'grad_mix_norm': 'grad_w', 'grad_w_in': 'grad_w', 'grad_sb_out_norm': 'grad_w', 'grad_dn_conv_w': 'grad_w', 'grad_dn_a_log': 'grad_w', 'grad_dn_dt_bias': 'grad_w', 'grad_dn_out_norm': 'grad_w', 'grad_s5_a_re': 'grad_w', 'grad_s5_a_im': 'grad_w', 'grad_s5_log_dt': 'grad_w', 'grad_s5_b_re': 'grad_w', 'grad_s5_b_im': 'grad_w', 'grad_s5_c_re': 'grad_w', 'grad_s5_c_im': 'grad_w', 'grad_s5_d': 'grad_w', 'grad_s5_w_glu': 'grad_w', 'grad_s5_b_glu': 'grad_w', 'grad_s5_out_norm': 'grad_w', 'grad_w_out': 'grad_w', 'grad_ffn2_norm': 'grad_w', 'grad_ffn2_w_gate': 'grad_w', 'grad_ffn2_w_up': 'grad_w', 'grad_ffn2_w_down': 'grad_w', 'grad_final_norm': 'grad_w', 'delta_meta_tokens': 'delta_w', 'delta_ffn1_norm': 'delta_w', 'delta_ffn1_w_gate': 'delta_w', 'delta_ffn1_w_up': 'delta_w', 'delta_ffn1_w_down': 'delta_w', 'delta_mix_norm': 'delta_w', 'delta_w_in': 'delta_w', 'delta_sb_out_norm': 'delta_w', 'delta_dn_conv_w': 'delta_w', 'delta_dn_a_log': 'delta_w', 'delta_dn_dt_bias': 'delta_w', 'delta_dn_out_norm': 'delta_w', 'delta_s5_a_re': 'delta_w', 'delta_s5_a_im': 'delta_w', 'delta_s5_log_dt': 'delta_w', 'delta_s5_b_re': 'delta_w', 'delta_s5_b_im': 'delta_w', 'delta_s5_c_re': 'delta_w', 'delta_s5_c_im': 'delta_w', 'delta_s5_d': 'delta_w', 'delta_s5_w_glu': 'delta_w', 'delta_s5_b_glu': 'delta_w', 'delta_s5_out_norm': 'delta_w', 'delta_w_out': 'delta_w', 'delta_ffn2_norm': 'delta_w', 'delta_ffn2_w_gate': 'delta_w', 'delta_ffn2_w_up': 'delta_w', 'delta_ffn2_w_down': 'delta_w', 'delta_final_norm': 'delta_w', 'new_m_meta_tokens': 'new_m', 'new_m_ffn1_norm': 'new_m', 'new_m_ffn1_w_gate': 'new_m', 'new_m_ffn1_w_up': 'new_m', 'new_m_ffn1_w_down': 'new_m', 'new_m_mix_norm': 'new_m', 'new_m_w_in': 'new_m', 'new_m_sb_out_norm': 'new_m', 'new_m_dn_conv_w': 'new_m', 'new_m_dn_a_log': 'new_m', 'new_m_dn_dt_bias': 'new_m', 'new_m_dn_out_norm': 'new_m', 'new_m_s5_a_re': 'new_m', 'new_m_s5_a_im': 'new_m', 'new_m_s5_log_dt': 'new_m', 'new_m_s5_b_re': 'new_m', 'new_m_s5_b_im': 'new_m', 'new_m_s5_c_re': 'new_m', 'new_m_s5_c_im': 'new_m', 'new_m_s5_d': 'new_m', 'new_m_s5_w_glu': 'new_m', 'new_m_s5_b_glu': 'new_m', 'new_m_s5_out_norm': 'new_m', 'new_m_w_out': 'new_m', 'new_m_ffn2_norm': 'new_m', 'new_m_ffn2_w_gate': 'new_m', 'new_m_ffn2_w_up': 'new_m', 'new_m_ffn2_w_down': 'new_m', 'new_m_final_norm': 'new_m', 'new_v_meta_tokens': 'new_v', 'new_v_ffn1_norm': 'new_v', 'new_v_ffn1_w_gate': 'new_v', 'new_v_ffn1_w_up': 'new_v', 'new_v_ffn1_w_down': 'new_v', 'new_v_mix_norm': 'new_v', 'new_v_w_in': 'new_v', 'new_v_sb_out_norm': 'new_v', 'new_v_dn_conv_w': 'new_v', 'new_v_dn_a_log': 'new_v', 'new_v_dn_dt_bias': 'new_v', 'new_v_dn_out_norm': 'new_v', 'new_v_s5_a_re': 'new_v', 'new_v_s5_a_im': 'new_v', 'new_v_s5_log_dt': 'new_v', 'new_v_s5_b_re': 'new_v', 'new_v_s5_b_im': 'new_v', 'new_v_s5_c_re': 'new_v', 'new_v_s5_c_im': 'new_v', 'new_v_s5_d': 'new_v', 'new_v_s5_w_glu': 'new_v', 'new_v_s5_b_glu': 'new_v', 'new_v_s5_out_norm': 'new_v', 'new_v_w_out': 'new_v', 'new_v_ffn2_norm': 'new_v', 'new_v_ffn2_w_gate': 'new_v', 'new_v_ffn2_w_up': 'new_v', 'new_v_ffn2_w_down': 'new_v', 'new_v_final_norm': 'new_v'}


def _forward(args):
    return _fwd_reference(*[args[k] for k in FWD_PARAMS])


def _output_shape():
    def fwd():
        inp = _fwd_setup_inputs(0)
        return _fwd_reference(*[inp[k] for k in FWD_PARAMS])
    out = _jax.eval_shape(fwd)
    return out.shape, out.dtype

N_MICROBATCH = 1
ADAM_LR = 0.001
ADAM_B1 = 0.9
ADAM_B2 = 0.999
ADAM_EPS = 1e-08
ADAM_WD = 0.01
ADAM_STEP = 10
PER_EXAMPLE_BATCH_AXIS = {'x': 0, 'loss_target': 0}
SHARED_INPUTS = []
_WEIGHT_DTYPES = {'meta_tokens': _jnp.float32, 'ffn1_norm': _jnp.float32, 'ffn1_w_gate': _jnp.float32, 'ffn1_w_up': _jnp.float32, 'ffn1_w_down': _jnp.float32, 'mix_norm': _jnp.float32, 'w_in': _jnp.float32, 'sb_out_norm': _jnp.float32, 'dn_conv_w': _jnp.float32, 'dn_a_log': _jnp.float32, 'dn_dt_bias': _jnp.float32, 'dn_out_norm': _jnp.float32, 's5_a_re': _jnp.float32, 's5_a_im': _jnp.float32, 's5_log_dt': _jnp.float32, 's5_b_re': _jnp.float32, 's5_b_im': _jnp.float32, 's5_c_re': _jnp.float32, 's5_c_im': _jnp.float32, 's5_d': _jnp.float32, 's5_w_glu': _jnp.float32, 's5_b_glu': _jnp.float32, 's5_out_norm': _jnp.float32, 'w_out': _jnp.float32, 'ffn2_norm': _jnp.float32, 'ffn2_w_gate': _jnp.float32, 'ffn2_w_up': _jnp.float32, 'ffn2_w_down': _jnp.float32, 'final_norm': _jnp.float32}
MOMENT_SCALE = {'meta_tokens': 8.682014e-03, 'ffn1_norm': 1.558759e-01, 'ffn1_w_gate': 6.585178e-02, 'ffn1_w_up': 6.382215e-02, 'ffn1_w_down': 1.060075e-01, 'mix_norm': 2.911147e-01, 'w_in': 1.930982e-01, 'sb_out_norm': 6.189930e-01, 'dn_conv_w': 1.448381e-01, 'dn_a_log': 3.901135e-01, 'dn_dt_bias': 3.839040e-01, 'dn_out_norm': 2.759346e-01, 's5_a_re': 2.021472e-02, 's5_a_im': 2.061273e-02, 's5_log_dt': 1.161265e+01, 's5_b_re': 1.280677e-02, 's5_b_im': 1.041507e-02, 's5_c_re': 2.237208e-02, 's5_c_im': 2.259570e-02, 's5_d': 3.755101e-01, 's5_w_glu': 7.585954e-02, 's5_b_glu': 1.391580e-01, 's5_out_norm': 3.554478e-01, 'w_out': 2.810839e-01, 'ffn2_norm': 1.052964e-01, 'ffn2_w_gate': 4.522748e-02, 'ffn2_w_up': 4.445478e-02, 'ffn2_w_down': 7.381940e-02, 'final_norm': 1.290911e+02}


def _to_microbatches(a, axis):
    t = _jnp.moveaxis(a, axis, 0)
    t = t.reshape((N_MICROBATCH, t.shape[0] // N_MICROBATCH) + t.shape[1:])
    return _jnp.moveaxis(t, 1, axis + 1)


def setup_inputs(seed: int = 0) -> dict:
    inp = _fwd_setup_inputs(seed)
    key = _jax.random.fold_in(_jax.random.key(seed), 7919)
    shape, _ = _output_shape()
    out = dict(inp)
    out["loss_target"] = _jax.random.normal(_jax.random.fold_in(key, 0), shape, _jnp.float32)
    for i, name in enumerate(TWIN_WEIGHTS):
        w = inp[name].astype(_jnp.float32)
        if MOMENT_SCALE is None:
            s = _jnp.sqrt(_jnp.mean(_jnp.square(w)) + 1e-30)
        else:
            s = MOMENT_SCALE[name]
        km, kv = _jax.random.split(_jax.random.fold_in(key, i + 1))
        out[name] = w
        out["m_" + name] = s * _jax.random.normal(km, w.shape, _jnp.float32)
        out["v_" + name] = (s * s) * _jax.random.uniform(kv, w.shape, _jnp.float32, 0.5, 1.5)
    if N_MICROBATCH > 1:
        for name, axis in PER_EXAMPLE_BATCH_AXIS.items():
            out[name] = _to_microbatches(out[name], axis)
    return {'x': out['x'], 'meta_tokens': out['meta_tokens'], 'ffn1_norm': out['ffn1_norm'], 'ffn1_w_gate': out['ffn1_w_gate'], 'ffn1_w_up': out['ffn1_w_up'], 'ffn1_w_down': out['ffn1_w_down'], 'mix_norm': out['mix_norm'], 'w_in': out['w_in'], 'sb_out_norm': out['sb_out_norm'], 'dn_conv_w': out['dn_conv_w'], 'dn_a_log': out['dn_a_log'], 'dn_dt_bias': out['dn_dt_bias'], 'dn_out_norm': out['dn_out_norm'], 's5_a_re': out['s5_a_re'], 's5_a_im': out['s5_a_im'], 's5_log_dt': out['s5_log_dt'], 's5_b_re': out['s5_b_re'], 's5_b_im': out['s5_b_im'], 's5_c_re': out['s5_c_re'], 's5_c_im': out['s5_c_im'], 's5_d': out['s5_d'], 's5_w_glu': out['s5_w_glu'], 's5_b_glu': out['s5_b_glu'], 's5_out_norm': out['s5_out_norm'], 'w_out': out['w_out'], 'ffn2_norm': out['ffn2_norm'], 'ffn2_w_gate': out['ffn2_w_gate'], 'ffn2_w_up': out['ffn2_w_up'], 'ffn2_w_down': out['ffn2_w_down'], 'final_norm': out['final_norm'], 'loss_target': out['loss_target'], 'm_meta_tokens': out['m_meta_tokens'], 'm_ffn1_norm': out['m_ffn1_norm'], 'm_ffn1_w_gate': out['m_ffn1_w_gate'], 'm_ffn1_w_up': out['m_ffn1_w_up'], 'm_ffn1_w_down': out['m_ffn1_w_down'], 'm_mix_norm': out['m_mix_norm'], 'm_w_in': out['m_w_in'], 'm_sb_out_norm': out['m_sb_out_norm'], 'm_dn_conv_w': out['m_dn_conv_w'], 'm_dn_a_log': out['m_dn_a_log'], 'm_dn_dt_bias': out['m_dn_dt_bias'], 'm_dn_out_norm': out['m_dn_out_norm'], 'm_s5_a_re': out['m_s5_a_re'], 'm_s5_a_im': out['m_s5_a_im'], 'm_s5_log_dt': out['m_s5_log_dt'], 'm_s5_b_re': out['m_s5_b_re'], 'm_s5_b_im': out['m_s5_b_im'], 'm_s5_c_re': out['m_s5_c_re'], 'm_s5_c_im': out['m_s5_c_im'], 'm_s5_d': out['m_s5_d'], 'm_s5_w_glu': out['m_s5_w_glu'], 'm_s5_b_glu': out['m_s5_b_glu'], 'm_s5_out_norm': out['m_s5_out_norm'], 'm_w_out': out['m_w_out'], 'm_ffn2_norm': out['m_ffn2_norm'], 'm_ffn2_w_gate': out['m_ffn2_w_gate'], 'm_ffn2_w_up': out['m_ffn2_w_up'], 'm_ffn2_w_down': out['m_ffn2_w_down'], 'm_final_norm': out['m_final_norm'], 'v_meta_tokens': out['v_meta_tokens'], 'v_ffn1_norm': out['v_ffn1_norm'], 'v_ffn1_w_gate': out['v_ffn1_w_gate'], 'v_ffn1_w_up': out['v_ffn1_w_up'], 'v_ffn1_w_down': out['v_ffn1_w_down'], 'v_mix_norm': out['v_mix_norm'], 'v_w_in': out['v_w_in'], 'v_sb_out_norm': out['v_sb_out_norm'], 'v_dn_conv_w': out['v_dn_conv_w'], 'v_dn_a_log': out['v_dn_a_log'], 'v_dn_dt_bias': out['v_dn_dt_bias'], 'v_dn_out_norm': out['v_dn_out_norm'], 'v_s5_a_re': out['v_s5_a_re'], 'v_s5_a_im': out['v_s5_a_im'], 'v_s5_log_dt': out['v_s5_log_dt'], 'v_s5_b_re': out['v_s5_b_re'], 'v_s5_b_im': out['v_s5_b_im'], 'v_s5_c_re': out['v_s5_c_re'], 'v_s5_c_im': out['v_s5_c_im'], 'v_s5_d': out['v_s5_d'], 'v_s5_w_glu': out['v_s5_w_glu'], 'v_s5_b_glu': out['v_s5_b_glu'], 'v_s5_out_norm': out['v_s5_out_norm'], 'v_w_out': out['v_w_out'], 'v_ffn2_norm': out['v_ffn2_norm'], 'v_ffn2_w_gate': out['v_ffn2_w_gate'], 'v_ffn2_w_up': out['v_ffn2_w_up'], 'v_ffn2_w_down': out['v_ffn2_w_down'], 'v_final_norm': out['v_final_norm']}


def _loss(weights, diff, rest, loss_target):
    with _jax.named_scope("forward"):
        args = {**rest, TWIN_DIFF_INPUT: diff, **{k: w.astype(_WEIGHT_DTYPES[k]) for k, w in weights.items()}}
        y = _forward(args)
    with _jax.named_scope("loss_head"):
        err = _jnp.square(y.astype(_jnp.float32) - loss_target)
        return 0.5 * _jnp.sum(_jnp.mean(err, axis=-1)) if err.ndim else 0.5 * err


def _adamw(w, g, m, v):
    m = ADAM_B1 * m + (1.0 - ADAM_B1) * g
    v = ADAM_B2 * v + (1.0 - ADAM_B2) * _jnp.square(g)
    m_hat = m / (1.0 - ADAM_B1 ** ADAM_STEP)
    v_hat = v / (1.0 - ADAM_B2 ** ADAM_STEP)
    delta = -ADAM_LR * (m_hat / (_jnp.sqrt(v_hat) + ADAM_EPS) + ADAM_WD * w)
    return delta, m, v


def reference(x, meta_tokens, ffn1_norm, ffn1_w_gate, ffn1_w_up, ffn1_w_down, mix_norm, w_in, sb_out_norm, dn_conv_w, dn_a_log, dn_dt_bias, dn_out_norm, s5_a_re, s5_a_im, s5_log_dt, s5_b_re, s5_b_im, s5_c_re, s5_c_im, s5_d, s5_w_glu, s5_b_glu, s5_out_norm, w_out, ffn2_norm, ffn2_w_gate, ffn2_w_up, ffn2_w_down, final_norm, loss_target, m_meta_tokens, m_ffn1_norm, m_ffn1_w_gate, m_ffn1_w_up, m_ffn1_w_down, m_mix_norm, m_w_in, m_sb_out_norm, m_dn_conv_w, m_dn_a_log, m_dn_dt_bias, m_dn_out_norm, m_s5_a_re, m_s5_a_im, m_s5_log_dt, m_s5_b_re, m_s5_b_im, m_s5_c_re, m_s5_c_im, m_s5_d, m_s5_w_glu, m_s5_b_glu, m_s5_out_norm, m_w_out, m_ffn2_norm, m_ffn2_w_gate, m_ffn2_w_up, m_ffn2_w_down, m_final_norm, v_meta_tokens, v_ffn1_norm, v_ffn1_w_gate, v_ffn1_w_up, v_ffn1_w_down, v_mix_norm, v_w_in, v_sb_out_norm, v_dn_conv_w, v_dn_a_log, v_dn_dt_bias, v_dn_out_norm, v_s5_a_re, v_s5_a_im, v_s5_log_dt, v_s5_b_re, v_s5_b_im, v_s5_c_re, v_s5_c_im, v_s5_d, v_s5_w_glu, v_s5_b_glu, v_s5_out_norm, v_w_out, v_ffn2_norm, v_ffn2_w_gate, v_ffn2_w_up, v_ffn2_w_down, v_final_norm):
    given = dict(x=x, meta_tokens=meta_tokens, ffn1_norm=ffn1_norm, ffn1_w_gate=ffn1_w_gate, ffn1_w_up=ffn1_w_up, ffn1_w_down=ffn1_w_down, mix_norm=mix_norm, w_in=w_in, sb_out_norm=sb_out_norm, dn_conv_w=dn_conv_w, dn_a_log=dn_a_log, dn_dt_bias=dn_dt_bias, dn_out_norm=dn_out_norm, s5_a_re=s5_a_re, s5_a_im=s5_a_im, s5_log_dt=s5_log_dt, s5_b_re=s5_b_re, s5_b_im=s5_b_im, s5_c_re=s5_c_re, s5_c_im=s5_c_im, s5_d=s5_d, s5_w_glu=s5_w_glu, s5_b_glu=s5_b_glu, s5_out_norm=s5_out_norm, w_out=w_out, ffn2_norm=ffn2_norm, ffn2_w_gate=ffn2_w_gate, ffn2_w_up=ffn2_w_up, ffn2_w_down=ffn2_w_down, final_norm=final_norm, loss_target=loss_target, m_meta_tokens=m_meta_tokens, m_ffn1_norm=m_ffn1_norm, m_ffn1_w_gate=m_ffn1_w_gate, m_ffn1_w_up=m_ffn1_w_up, m_ffn1_w_down=m_ffn1_w_down, m_mix_norm=m_mix_norm, m_w_in=m_w_in, m_sb_out_norm=m_sb_out_norm, m_dn_conv_w=m_dn_conv_w, m_dn_a_log=m_dn_a_log, m_dn_dt_bias=m_dn_dt_bias, m_dn_out_norm=m_dn_out_norm, m_s5_a_re=m_s5_a_re, m_s5_a_im=m_s5_a_im, m_s5_log_dt=m_s5_log_dt, m_s5_b_re=m_s5_b_re, m_s5_b_im=m_s5_b_im, m_s5_c_re=m_s5_c_re, m_s5_c_im=m_s5_c_im, m_s5_d=m_s5_d, m_s5_w_glu=m_s5_w_glu, m_s5_b_glu=m_s5_b_glu, m_s5_out_norm=m_s5_out_norm, m_w_out=m_w_out, m_ffn2_norm=m_ffn2_norm, m_ffn2_w_gate=m_ffn2_w_gate, m_ffn2_w_up=m_ffn2_w_up, m_ffn2_w_down=m_ffn2_w_down, m_final_norm=m_final_norm, v_meta_tokens=v_meta_tokens, v_ffn1_norm=v_ffn1_norm, v_ffn1_w_gate=v_ffn1_w_gate, v_ffn1_w_up=v_ffn1_w_up, v_ffn1_w_down=v_ffn1_w_down, v_mix_norm=v_mix_norm, v_w_in=v_w_in, v_sb_out_norm=v_sb_out_norm, v_dn_conv_w=v_dn_conv_w, v_dn_a_log=v_dn_a_log, v_dn_dt_bias=v_dn_dt_bias, v_dn_out_norm=v_dn_out_norm, v_s5_a_re=v_s5_a_re, v_s5_a_im=v_s5_a_im, v_s5_log_dt=v_s5_log_dt, v_s5_b_re=v_s5_b_re, v_s5_b_im=v_s5_b_im, v_s5_c_re=v_s5_c_re, v_s5_c_im=v_s5_c_im, v_s5_d=v_s5_d, v_s5_w_glu=v_s5_w_glu, v_s5_b_glu=v_s5_b_glu, v_s5_out_norm=v_s5_out_norm, v_w_out=v_w_out, v_ffn2_norm=v_ffn2_norm, v_ffn2_w_gate=v_ffn2_w_gate, v_ffn2_w_up=v_ffn2_w_up, v_ffn2_w_down=v_ffn2_w_down, v_final_norm=v_final_norm)
    weights = {n: given[n] for n in TWIN_WEIGHTS}
    shared = {n: given[n] for n in SHARED_INPUTS}
    per_example = {n: given[n] for n in ['x']}
    grad_fn = _jax.value_and_grad(_loss, argnums=(0, 1))

    def one_microbatch(ex, loss_target):
        ex = dict(ex)
        diff = ex.pop(TWIN_DIFF_INPUT)
        return grad_fn(weights, diff, {**shared, **ex}, loss_target)

    if N_MICROBATCH == 1:
        loss, (grad_w, grad_x) = one_microbatch(per_example, given["loss_target"])
    else:
        def body(carry, xs):
            loss_sum, grad_sum = carry
            l_k, (gw_k, gx_k) = one_microbatch(xs[0], xs[1])
            with _jax.named_scope("update"):
                return (loss_sum + l_k, _jax.tree.map(_jnp.add, grad_sum, gw_k)), gx_k

        init = (_jnp.zeros((), _jnp.float32), _jax.tree.map(_jnp.zeros_like, weights))
        (loss, grad_w), grad_x = _jax.lax.scan(body, init, (per_example, given["loss_target"]))
    with _jax.named_scope("update"):
        delta_w, new_m, new_v = {}, {}, {}
        for n in TWIN_WEIGHTS:
            delta_w[n], new_m[n], new_v[n] = _adamw(weights[n], grad_w[n], given["m_" + n], given["v_" + n])
    return (loss, grad_x, *[grad_w[n] for n in TWIN_WEIGHTS], *[delta_w[n] for n in TWIN_WEIGHTS],
            *[new_m[n] for n in TWIN_WEIGHTS], *[new_v[n] for n in TWIN_WEIGHTS])
```

```python
import functools
import math

import jax
import jax.numpy as jnp
from jax import lax
from jax.experimental import pallas as pl
from jax.experimental.pallas import tpu as pltpu

f32 = jnp.float32
bf16 = jnp.bfloat16
HI = lax.Precision.HIGHEST
SDS = jax.ShapeDtypeStruct

N_DEV = 8
D_MODEL = 1024
N_META = 16
PAD = 240
HEAD_DIM = 64
N_HEADS = 4
GW = N_HEADS * HEAD_DIM
DN_CONV = 4
S5_WIDTH = 512
S5_GROUP = 16
S5_GROUPS = 32
S5_STATE = 64
S5_LANES = S5_GROUPS * S5_STATE
D_FF = 2816
DEPTH = 2
EPS = 1e-6
C_SBQ, C_SBK, C_SBV, C_DNQKV, C_DNZ, C_DNBA, C_S5U, IN_PAD = 0, 256, 512, 768, 1536, 1792, 1920, 2560
IN_WIDTH = 2312
IN_SMALL = 1800
VMEM_LIMIT = 56 * 1024 * 1024
PACK_COLS = 512

ADAM_LR, ADAM_B1, ADAM_B2, ADAM_EPS, ADAM_WD, ADAM_STEP = 0.001, 0.9, 0.999, 1e-08, 0.01, 10


def _pick(n, cands):
    for c in cands:
        if n % c == 0:
            return c
    return n


def _cparams():
    return pltpu.CompilerParams(vmem_limit_bytes=VMEM_LIMIT)


_DIMS = {"nn": (((1,), (0,)), ((), ())), "nt": (((1,), (1,)), ((), ())), "tn": (((0,), (0,)), ((), ()))}


def _dot16(a, b, mode="nn"):
    return lax.dot_general(a.astype(bf16), b.astype(bf16), _DIMS[mode], preferred_element_type=f32)


def _make_bdot(mode):
    @jax.custom_vjp
    def f(a, b):
        return _dot16(a, b, mode)

    def fwd(a, b):
        return _dot16(a, b, mode), (a, b)

    def bwd(res, g):
        a, b = res
        if mode == "nn":
            return _dot16(g, b, "nt"), _dot16(a, g, "tn")
        if mode == "nt":
            return _dot16(g, b, "nn"), _dot16(g, a, "tn")
        return _dot16(b, g, "nt"), _dot16(a, g, "nn")

    f.defvjp(fwd, bwd)
    return f


bdot = _make_bdot("nn")
bdot_nt = _make_bdot("nt")
bdot_tn = _make_bdot("tn")


def _split3(x):
    h = x.astype(bf16)
    r = x - h.astype(f32)
    m = r.astype(bf16)
    l = (r - m.astype(f32)).astype(bf16)
    return h, m, l


def _dot3(a, b, mode="nn"):
    ah, am, al = _split3(a)
    bh, bm, bl = _split3(b)
    d = lambda x, y: lax.dot_general(x, y, _DIMS[mode], preferred_element_type=f32)
    return ((d(al, bh) + d(ah, bl)) + d(am, bm)) + ((d(am, bh) + d(ah, bm)) + d(ah, bh))


def _make_xdot(dot):
    @jax.custom_vjp
    def f(a, b):
        return dot(a, b, "nn")

    def fwd(a, b):
        return dot(a, b, "nn"), (a, b)

    def bwd(res, g):
        a, b = res
        return dot(g, b, "nt"), dot(a, g, "tn")

    f.defvjp(fwd, bwd)
    return f


xdot = _make_xdot(_dot3)


def _sel_dot(sel, x, mode):
    s = sel.astype(bf16)
    h, m, l = _split3(x)
    d = lambda y: lax.dot_general(s, y, _DIMS[mode], preferred_element_type=f32)
    return (d(l) + d(m)) + d(h)


@jax.custom_vjp
def _select_rows(sel, x):
    return _sel_dot(sel, x, "nn")


def _select_rows_fwd(sel, x):
    return _sel_dot(sel, x, "nn"), sel


def _select_rows_bwd(sel, g):
    return jnp.zeros_like(sel), _sel_dot(sel, g, "tn")


_select_rows.defvjp(_select_rows_fwd, _select_rows_bwd)


def _matmul(a, b, mode, name, out_dtype=f32, scale=None, res=None):
    row_c = (1280, 640, 512, 384, 256, 128)
    col_c = (1408, 1280, 1024, 512, 256, 128)
    if mode == "nn":
        (m, k), n = a.shape, b.shape[1]
        bo1, bo2, br = _pick(m, row_c), _pick(n, col_c), _pick(k, (1024, 1408, 1280, 512, 256, 128))
        out, red = (m, n), k
        a_spec = pl.BlockSpec((bo1, br), lambda i, j, r: (i, r))
        b_spec = pl.BlockSpec((br, bo2), lambda i, j, r: (r, j))
    elif mode == "nt":
        (m, n), k = a.shape, b.shape[0]
        bo1, bo2, br = _pick(m, row_c), _pick(k, col_c), _pick(n, (1408, 1280, 1024, 512, 256, 128))
        out, red = (m, k), n
        a_spec = pl.BlockSpec((bo1, br), lambda i, j, r: (i, r))
        b_spec = pl.BlockSpec((bo2, br), lambda i, j, r: (j, r))
    else:
        (m, k), n = a.shape, b.shape[1]
        bo1, bo2, br = _pick(k, (1024, 1408, 1280, 512, 256, 128)), _pick(n, col_c), _pick(m, row_c[1:])
        out, red = (k, n), m
        a_spec = pl.BlockSpec((br, bo1), lambda i, j, r: (r, i))
        b_spec = pl.BlockSpec((br, bo2), lambda i, j, r: (r, j))
    nred = red // br
    o_spec = pl.BlockSpec((bo1, bo2), lambda i, j, r: (i, j))

    def body(a_ref, b_ref, *rest):
        res_ref = rest[0] if res is not None else None
        o_ref, acc_ref = rest[-2:]
        r = pl.program_id(2)

        @pl.when(r == 0)
        def _():
            acc_ref[...] = jnp.zeros_like(acc_ref)

        acc_ref[...] += _dot16(a_ref[...], b_ref[...], mode)

        @pl.when(r == nred - 1)
        def _():
            y = acc_ref[...]
            if scale is not None:
                y = y * scale
            if res_ref is not None:
                y = y + res_ref[...].astype(f32)
            o_ref[...] = y.astype(out_dtype)

    operands = (a, b) if res is None else (a, b, res)
    return pl.pallas_call(
        body, grid=(out[0] // bo1, out[1] // bo2, nred), in_specs=[a_spec, b_spec] + ([o_spec] if res is not None else []),
        out_specs=o_spec, out_shape=SDS(out, out_dtype),
        scratch_shapes=[pltpu.VMEM((bo1, bo2), f32)], name=name, compiler_params=_cparams())(*operands)


def _make_linear(name, out_dtype=f32, scale=None):
    def run(x, w, h):
        return _matmul(x, w.astype(bf16), "nn", name + "_fwd", out_dtype=out_dtype, scale=scale, res=h)

    def grads(x, w, dy):
        dx = _matmul(dy, w.astype(bf16), "nt", name + "_dx", out_dtype=x.dtype, scale=scale)
        return dx, _matmul(x, dy, "tn", name + "_dw", scale=scale)

    @jax.custom_vjp
    def lin(x, w):
        return run(x, w, None)

    lin.defvjp(lambda x, w: (run(x, w, None), (x, w)), lambda res, dy: grads(*res, dy))

    @jax.custom_vjp
    def lin_res(x, w, h):
        return run(x, w, h)

    lin_res.defvjp(lambda x, w, h: (run(x, w, h), (x, w)), lambda res, dy: (*grads(*res, dy), dy))
    return lin, lin_res


def _make_blockop(f, name, out_cols, tm_cands, out_dtypes=None, whole_lead=False):
    out_dtypes = out_dtypes or (f32,) * len(out_cols)

    def specs(arrs, tm, lead, g=None):
        cols = [a if isinstance(a, int) else a.shape[-1] for a in arrs]
        if g is not None:
            return [pl.BlockSpec((g, tm, c), lambda i: (0, i, 0)) for c in cols]
        if lead:
            return [pl.BlockSpec((None, tm, c), lambda g, i: (g, i, 0)) for c in cols]
        return [pl.BlockSpec((tm, c), lambda i: (i, 0)) for c in cols]

    def pspecs(params, lead):
        if lead:
            return [pl.BlockSpec(p.shape, lambda g, i: (0, 0)) for p in params]
        return [pl.BlockSpec(p.shape, lambda i: (0, 0)) for p in params]

    def geometry(ins):
        lead = ins[0].ndim == 3 and not whole_lead
        g = ins[0].shape[0] if ins[0].ndim == 3 and whole_lead else None
        ln = ins[0].shape[-2]
        tm = _pick(ln, tm_cands)
        grid = (ins[0].shape[0], ln // tm) if lead else (ln // tm,)
        return lead, g, tm, grid

    def fwd_call(ins, params):
        lead, g, tm, grid = geometry(ins)
        n_in, n_p = len(ins), len(params)

        def body(*refs):
            rowid = pl.program_id(1 if lead else 0) * tm + lax.broadcasted_iota(jnp.int32, (tm, 1), 0)
            outs = f(rowid, *[r[...].astype(f32) for r in refs[:n_in + n_p]])
            for o_ref, o in zip(refs[n_in + n_p:], outs):
                o_ref[...] = o.astype(o_ref.dtype)

        return pl.pallas_call(
            body, grid=grid, in_specs=specs(ins, tm, lead, g) + pspecs(params, lead),
            out_specs=specs(out_cols, tm, lead, g),
            out_shape=[SDS(ins[0].shape[:-1] + (c,), dt) for c, dt in zip(out_cols, out_dtypes)],
            name=name + "_fwd", compiler_params=_cparams())(*ins, *params)

    def bwd_call(ins, params, cts):
        lead, g, tm, grid = geometry(ins)
        n_in, n_p, n_o = len(ins), len(params), len(cts)

        def body(*refs):
            in_refs = refs[:n_in + n_p]
            ct_refs = refs[n_in + n_p:n_in + n_p + n_o]
            din_refs = refs[n_in + n_p + n_o:n_in + n_p + n_o + n_in]
            dp_refs = refs[n_in + n_p + n_o + n_in:]
            rowid = pl.program_id(1 if lead else 0) * tm + lax.broadcasted_iota(jnp.int32, (tm, 1), 0)
            _, vjp = jax.vjp(lambda *a: tuple(f(rowid, *a)), *[r[...].astype(f32) for r in in_refs])
            grads = vjp(tuple(r[...].astype(f32) for r in ct_refs))
            for r, g in zip(din_refs, grads[:n_in]):
                r[...] = g.astype(r.dtype)
            if n_p:
                first = (pl.program_id(0) == 0) & (pl.program_id(1) == 0) if lead else pl.program_id(0) == 0

                @pl.when(first)
                def _():
                    for r in dp_refs:
                        r[...] = jnp.zeros_like(r)

                for r, g in zip(dp_refs, grads[n_in:]):
                    r[...] += g

        return pl.pallas_call(
            body, grid=grid,
            in_specs=specs(ins, tm, lead, g) + pspecs(params, lead) + specs(cts, tm, lead, g),
            out_specs=specs(ins, tm, lead, g) + pspecs(params, lead),
            out_shape=[SDS(a.shape, a.dtype) for a in ins] + [SDS(p.shape, f32) for p in params],
            name=name + "_bwd", compiler_params=_cparams())(*ins, *params, *cts)

    @jax.custom_vjp
    def op(ins, params):
        return tuple(fwd_call(ins, params))

    def op_fwd(ins, params):
        return tuple(fwd_call(ins, params)), (ins, params)

    def op_bwd(res, cts):
        ins, params = res
        g = bwd_call(ins, params, cts)
        return tuple(g[:len(ins)]), tuple(g[len(ins):])

    op.defvjp(op_fwd, op_bwd)
    return op


def _rowmask(rowid):
    return (rowid >= PAD).astype(f32)


def _rms(x, g):
    return x * lax.rsqrt(jnp.mean(x * x, axis=-1, keepdims=True) + EPS) * g


def _group_mean_sq(x):
    w = x.shape[-1]
    r = lax.broadcasted_iota(jnp.int32, (w, w), 0) // HEAD_DIM
    c = lax.broadcasted_iota(jnp.int32, (w, w), 1) // HEAD_DIM
    return xdot(x * x, jnp.where(r == c, 1.0 / HEAD_DIM, 0.0).astype(f32))


def _f_rmsnorm(rowid, h, g):
    return (_rms(h, g) * _rowmask(rowid),)


def _f_swiglu(rowid, gu):
    half = gu.shape[-1] // 2
    return (jax.nn.silu(gu[:, :half]) * gu[:, half:],)


def _f_headnorm(rowid, o, g):
    return (o * lax.rsqrt(_group_mean_sq(o) + EPS) * g,)


def _f_dn_out(rowid, o, z, g):
    return (o * lax.rsqrt(_group_mean_sq(o) + EPS) * g * jax.nn.silu(z),)


def _f_dn_prep(rowid, conv, ba, alog, dtb):
    tm = conv.shape[0]
    mask = _rowmask(rowid)
    s = jax.nn.silu(conv)
    q, k, v = s[:, :GW], s[:, GW:2 * GW], s[:, 2 * GW:]
    q = q * lax.rsqrt(_group_mean_sq(q) * HEAD_DIM + EPS)
    k = k * lax.rsqrt(_group_mean_sq(k) * HEAD_DIM + EPS)
    beta = jax.nn.sigmoid(ba) * mask
    g = -jnp.exp(alog) * jax.nn.softplus(ba + dtb) * mask
    r = lax.broadcasted_iota(jnp.int32, (tm, tm), 0)
    c = lax.broadcasted_iota(jnp.int32, (tm, tm), 1)
    ltri = jnp.where((r >= c) & (r // 64 == c // 64), 1.0, 0.0).astype(f32)
    gc = xdot(ltri, g)
    er = lax.broadcasted_iota(jnp.int32, (128, GW), 0)
    ec = lax.broadcasted_iota(jnp.int32, (128, GW), 1) // HEAD_DIM
    e_b = jnp.where(er == ec, 1.0, 0.0).astype(f32)
    e_g = jnp.where(er == ec + N_HEADS, 1.0, 0.0).astype(f32)
    return q * mask, k * mask, v * mask, xdot(gc, e_g), xdot(beta, e_b)


def _f_dn_intra(rowid, q, k, v, gc, bb):
    tm = q.shape[0]
    r = lax.broadcasted_iota(jnp.int32, (tm, tm), 0)
    c = lax.broadcasted_iota(jnp.int32, (tm, tm), 1)
    same = r // 64 == c // 64
    incl = same & (r >= c)
    strict = same & (r > c)
    eye = (r == c).astype(f32)
    gcb = jnp.broadcast_to(gc[:, 0:1], (tm, tm))
    gcr = jnp.sum(gcb * eye, axis=0, keepdims=True)
    decay = jnp.where(incl, jnp.exp(jnp.where(incl, gcb - gcr, 0.0)), 0.0)
    qs = q * (HEAD_DIM ** -0.5)
    kb = k * bb
    lmat = jnp.where(strict, bdot_nt(kb, k) * decay, 0.0)
    t = eye - lmat
    p = lmat
    for _ in range(5):
        p = bdot(p, p)
        t = t + bdot(t, p)
    egc = jnp.exp(gc)
    u = bdot(t, v * bb)
    w = bdot(t, kb * egc)
    attn_big = jnp.where(incl, bdot_nt(qs, k) * decay, 0.0)
    attn = attn_big[:, 0:64]
    for b in range(1, tm // 64):
        attn = attn + attn_big[:, 64 * b:64 * b + 64]
    sel = jnp.where(c == (r // 64) * 64 + 63, 1.0, 0.0).astype(f32)
    gl = _select_rows(sel, gc)
    return u, w, qs * egc, k * jnp.exp(gl - gc), attn, jnp.exp(gl)


def _f_dn_intra_heads(rowid, *xs):
    per_head = [_f_dn_intra(rowid, *[x[h] for x in xs]) for h in range(xs[0].shape[0])]
    return tuple(jnp.stack([o[j] for o in per_head], axis=0) for j in range(len(per_head[0])))


def _f_s5_param(rowid, ar, ai, ldt, bre, bim):
    dt = jnp.exp(ldt)
    mag = jnp.exp(ar * dt)
    abr, abi = mag * jnp.cos(ai * dt), mag * jnp.sin(ai * dt)
    den = ar * ar + ai * ai
    nr = abr - 1.0
    qr = (nr * ar + abi * ai) / den
    qi = (abi * ar - nr * ai) / den
    return abr, abi, qr * bre - qi * bim, qr * bim + qi * bre


def _f_s5_post(rowid, y, u, d, wglu, bglu, gnorm):
    y = jax.nn.gelu(y + d * u)
    o = y * jax.nn.sigmoid(bdot(y, wglu) + bglu)
    return (_rms(o, gnorm),)


def _shift_down(x, halo, r, row8):
    if r == 0:
        return x
    rolled = pltpu.roll(x, r, 0)
    top = jnp.where(row8 < r, pltpu.roll(halo, r, 0), rolled[:8])
    return jnp.concatenate([top, rolled[8:]], axis=0)


def _shift_up(x, halo, r, row8):
    if r == 0:
        return x
    tm = x.shape[0]
    rolled = pltpu.roll(x, tm - r, 0)
    bot = jnp.where(row8 >= 8 - r, pltpu.roll(halo, 8 - r, 0), rolled[tm - 8:])
    return jnp.concatenate([rolled[:tm - 8], bot], axis=0)


def _conv_call(x, w8, mode, name, dc=None):
    ln, ch = x.shape
    tm = _pick(ln, (640, 256, 128))
    n = ln // tm
    t8 = tm // 8

    def body(*refs):
        i = pl.program_id(0)
        row8 = lax.broadcasted_iota(jnp.int32, (8, ch), 0)
        if mode == "fwd":
            x_ref, h_ref, w_ref, o_ref = refs
            halo = jnp.where(i > 0, h_ref[...], 0.0)
            xv = x_ref[...]
            acc = jnp.zeros((tm, ch), f32)
            for j in range(DN_CONV):
                acc = acc + w_ref[j:j + 1, :] * _shift_down(xv, halo, DN_CONV - 1 - j, row8)
            o_ref[...] = acc
        elif mode == "dx":
            x_ref, h_ref, w_ref, o_ref = refs
            halo = jnp.where(i < n - 1, h_ref[...], 0.0)
            xv = x_ref[...]
            acc = jnp.zeros((tm, ch), f32)
            for j in range(DN_CONV):
                acc = acc + w_ref[j:j + 1, :] * _shift_up(xv, halo, DN_CONV - 1 - j, row8)
            o_ref[...] = acc
        else:
            x_ref, h_ref, dc_ref, o_ref = refs
            halo = jnp.where(i > 0, h_ref[...], 0.0)
            xv, dcv = x_ref[...], dc_ref[...]
            acc = jnp.zeros((8, ch), f32)
            for j in range(DN_CONV):
                s = jnp.sum(_shift_down(xv, halo, DN_CONV - 1 - j, row8) * dcv, axis=0, keepdims=True)
                acc = acc + jnp.where(row8 == j, s, 0.0)

            @pl.when(i == 0)
            def _():
                o_ref[...] = jnp.zeros_like(o_ref)

            o_ref[...] += acc

    blk = pl.BlockSpec((tm, ch), lambda i: (i, 0))
    if mode == "dx":
        halo_spec = pl.BlockSpec((8, ch), lambda i: (jnp.minimum((i + 1) * t8, n * t8 - 1), 0))
    else:
        halo_spec = pl.BlockSpec((8, ch), lambda i: (jnp.maximum(i * t8 - 1, 0), 0))
    small = pl.BlockSpec((8, ch), lambda i: (0, 0))
    if mode == "dw":
        return pl.pallas_call(body, grid=(n,), in_specs=[blk, halo_spec, blk], out_specs=small,
                              out_shape=SDS((8, ch), f32), name=name, compiler_params=_cparams())(x, x, dc)
    return pl.pallas_call(body, grid=(n,), in_specs=[blk, halo_spec, small], out_specs=blk,
                          out_shape=SDS((ln, ch), f32), name=name, compiler_params=_cparams())(x, x, w8)


@jax.custom_vjp
def conv_op(x, w8):
    return _conv_call(x, w8, "fwd", "dn_conv_fwd")


def _conv_fwd(x, w8):
    return _conv_call(x, w8, "fwd", "dn_conv_fwd"), (x, w8)


def _conv_bwd(res, dc):
    x, w8 = res
    return _conv_call(dc, w8, "dx", "dn_conv_dx"), _conv_call(x, None, "dw", "dn_conv_dw", dc=dc)


conv_op.defvjp(_conv_fwd, _conv_bwd)


SB_DEAD = -104.0
SB_UNSEEN = -1e30


def _softplus(z):
    return jnp.maximum(z, 0.0) + jnp.log1p(jnp.exp(-jnp.abs(z)))


def _scan_matrix(suffix):
    r = lax.broadcasted_iota(jnp.int32, (256, 256), 0) % 128
    c = lax.broadcasted_iota(jnp.int32, (256, 256), 1)
    inside = (r > c) if suffix else (r < c)
    return jnp.where((c >= 128) | inside, 1.0, 0.0).astype(bf16)


def _block_sums(x, mat):
    hi = x.astype(bf16)
    lo = (x - hi.astype(f32)).astype(bf16)
    r = jnp.dot(jnp.concatenate([hi, lo], axis=1), mat, preferred_element_type=f32)
    return r[:, :128], r[:, 128:]


def _sb_scores(q, k, k0, qpos, suf_mat, c_lk):
    z = lax.dot_general(q, k, _DIMS["nt"], preferred_element_type=f32)
    kpos = k0 + lax.broadcasted_iota(jnp.int32, z.shape, 1)
    valid = (kpos < qpos) & (kpos >= PAD)
    sp = _softplus(z)
    lk = jnp.where(valid, -sp, 0.0)
    suf, tot = _block_sums(lk, suf_mat)
    w = jnp.where(valid, jnp.exp(z - sp + suf + c_lk), 0.0)
    return z, sp, valid, w, tot


def _sb_fwd_call(qs, kb, vb):
    nh, ln, hd = qs.shape
    tq = _pick(ln, (256, 128))
    nsub = tq // 128
    assert nsub == 2
    assert ln // 128 <= 256

    def body(q_ref, k_ref, v_ref, o_ref, c_ref):
        qi = pl.program_id(1)
        q = q_ref[...]
        suf_mat = _scan_matrix(True)
        qpos = qi * tq + lax.broadcasted_iota(jnp.int32, (tq, 128), 0)
        lane = lax.broadcasted_iota(jnp.int32, (tq, 256), 1)

        def live(carry):
            return (carry[0] >= 0) & (carry[1] > 0)

        def step(carry):
            t, _, c_lk, acc, saved = carry
            for tt in (t, t - 1):
                k0 = pl.multiple_of(tt * 128, 128)
                k = k_ref[pl.ds(k0, 128), :]
                v = v_ref[pl.ds(k0, 128), :]
                saved = jnp.where(lane == tt, jnp.concatenate([c_lk, c_lk], axis=1), saved)
                _, _, _, w, tot = _sb_scores(q, k, k0, qpos, suf_mat, c_lk)
                acc = acc + jnp.dot(w.astype(bf16), v, preferred_element_type=f32)
                c_lk = c_lk + tot
            return t - 2, (jnp.max(c_lk) >= SB_DEAD).astype(jnp.int32), c_lk, acc, saved

        init = ((qi + 1) * nsub - 1, jnp.int32(1), jnp.zeros((tq, 128), f32), jnp.zeros((tq, hd), f32),
                jnp.full((tq, 256), SB_UNSEEN, f32))
        _, _, _, acc, saved = lax.while_loop(live, step, init)
        o_ref[...] = acc
        c_ref[...] = saved

    full = pl.BlockSpec((None, ln, hd), lambda h, i: (h, 0, 0))
    blk = pl.BlockSpec((None, tq, hd), lambda h, i: (h, i, 0))
    cblk = pl.BlockSpec((None, tq, 256), lambda h, i: (h, i, 0))
    return pl.pallas_call(body, grid=(nh, ln // tq), in_specs=[blk, full, full], out_specs=[blk, cblk],
                          out_shape=[SDS((nh, ln, hd), f32), SDS((nh, ln, 256), f32)], name="sb_attn_fwd",
                          compiler_params=_cparams())(qs, kb, vb)


def _sb_bwd_call(qs, kb, vb, carries, do):
    nh, ln, hd = qs.shape
    tq = _pick(ln, (256, 128))
    nsub = tq // 128
    assert nsub == 2
    nq = ln // tq

    def body(q_ref, k_ref, v_ref, c_ref, do_ref, dq_ref, dk_hbm, dv_hbm, dk_ref, dv_ref):
        qi = pl.program_id(1)

        @pl.when(qi == 0)
        def _():
            dk_ref[...] = jnp.zeros_like(dk_ref)
            dv_ref[...] = jnp.zeros_like(dv_ref)

        q = q_ref[...]
        dob = do_ref[...].astype(bf16)
        saved = c_ref[...]
        suf_mat = _scan_matrix(True)
        pre_mat = _scan_matrix(False)
        qpos = qi * tq + lax.broadcasted_iota(jnp.int32, (tq, 128), 0)
        lane = lax.broadcasted_iota(jnp.int32, (tq, 256), 1)

        def step(i, carry):
            c_e, dq = carry
            for t in (2 * i, 2 * i + 1):
                k0 = pl.multiple_of(t * 128, 128)
                k = k_ref[pl.ds(k0, 128), :]
                v = v_ref[pl.ds(k0, 128), :]
                c_lk = jnp.sum(jnp.where(lane == t, saved, 0.0), axis=1, keepdims=True)
                z, sp, valid, w, _ = _sb_scores(q, k, k0, qpos, suf_mat, c_lk)
                e = lax.dot_general(dob, v, _DIMS["nt"], preferred_element_type=f32) * w
                pre_e, tot_e = _block_sums(e, pre_mat)
                sig = jnp.exp(z - sp)
                dz = jnp.where(valid, e * (1.0 - sig) - sig * (pre_e + c_e), 0.0).astype(bf16)
                dq = dq + jnp.dot(dz, k, preferred_element_type=f32)
                dk_ref[pl.ds(k0, 128), :] += lax.dot_general(dz, q, _DIMS["tn"], preferred_element_type=f32)
                dv_ref[pl.ds(k0, 128), :] += lax.dot_general(w.astype(bf16), dob, _DIMS["tn"], preferred_element_type=f32)
                c_e = c_e + tot_e
            return c_e, dq

        ntile = (qi + 1) * nsub
        lane1 = lax.broadcasted_iota(jnp.int32, (1, 256), 1)
        dead = (jnp.max(saved, axis=0, keepdims=True) < SB_DEAD) & (lane1 < ntile)
        first = jnp.sum(dead.astype(jnp.int32))
        _, dq = lax.fori_loop(first // 2, ntile // 2, step, (jnp.zeros((tq, 128), f32), jnp.zeros((tq, hd), f32)))
        dq_ref[...] = dq * (HEAD_DIM ** -0.5)

        @pl.when(qi == nq - 1)
        def _():
            pltpu.sync_copy(dk_ref, dk_hbm.at[pl.program_id(0)])
            pltpu.sync_copy(dv_ref, dv_hbm.at[pl.program_id(0)])

    full = pl.BlockSpec((None, ln, hd), lambda h, i: (h, 0, 0))
    blk = pl.BlockSpec((None, tq, hd), lambda h, i: (h, i, 0))
    cblk = pl.BlockSpec((None, tq, 256), lambda h, i: (h, i, 0))
    anyspec = pl.BlockSpec(memory_space=pl.ANY)
    return pl.pallas_call(body, grid=(nh, nq), in_specs=[blk, full, full, cblk, blk],
                          out_specs=[blk, anyspec, anyspec], out_shape=[SDS((nh, ln, hd), f32)] * 3,
                          scratch_shapes=[pltpu.VMEM((ln, hd), f32)] * 2,
                          name="sb_attn_bwd", compiler_params=_cparams())(qs, kb, vb, carries, do)


def _sb_operands(q, k, v):
    return (q * (HEAD_DIM ** -0.5)).astype(bf16), k.astype(bf16), v.astype(bf16)


@jax.custom_vjp
def sb_attention(q, k, v):
    return _sb_fwd_call(*_sb_operands(q, k, v))[0]


def _sb_fwd(q, k, v):
    qs, kb, vb = _sb_operands(q, k, v)
    o, carries = _sb_fwd_call(qs, kb, vb)
    return o, (qs, kb, vb, carries)


def _sb_bwd(res, do):
    return tuple(_sb_bwd_call(*res, do))


sb_attention.defvjp(_sb_fwd, _sb_bwd)


def _dn_scan_geometry(ln):
    tb = _pick(ln, (640, 256))
    return tb, ln // tb, tb // 64


def _dn_scan_fwd_call(u, w, qg, kg, attn, egl):
    nh, ln, hd = u.shape
    tb, nblk, nck = _dn_scan_geometry(ln)

    def body(u_ref, w_ref, qg_ref, kg_ref, a_ref, e_ref, o_ref, hist_ref, s_ref):
        @pl.when(pl.program_id(0) == 0)
        def _():
            s_ref[...] = jnp.zeros_like(s_ref)

        def chunk(ci, _):
            rows = pl.ds(pl.multiple_of(ci * 64, 64), 64)
            for h in range(nh):
                s = s_ref[h]
                hist_ref[ci, h] = s
                v_new = u_ref[h, rows, :] - _dot16(w_ref[h, rows, :], s)
                o_ref[h, rows, :] = _dot16(qg_ref[h, rows, :], s) + _dot16(a_ref[h, rows, :], v_new)
                s_ref[h] = s * e_ref[h, rows, :][0:1, :] + _dot16(kg_ref[h, rows, :], v_new, "tn")
            return 0

        lax.fori_loop(0, nck, chunk, 0)

    blk = pl.BlockSpec((nh, tb, hd), lambda i: (0, i, 0))
    return pl.pallas_call(
        body, grid=(nblk,), in_specs=[blk] * 6,
        out_specs=[blk, pl.BlockSpec((nck, nh, hd, hd), lambda i: (i, 0, 0, 0))],
        out_shape=[SDS((nh, ln, hd), f32), SDS((ln // 64, nh, hd, hd), f32)],
        scratch_shapes=[pltpu.VMEM((nh, hd, hd), f32)], name="dn_scan_fwd", compiler_params=_cparams())(u, w, qg, kg, attn, egl)


def _dn_scan_bwd_call(u, w, qg, kg, attn, egl, hist, do):
    nh, ln, hd = u.shape
    tb, nblk, nck = _dn_scan_geometry(ln)

    def body(u_ref, w_ref, qg_ref, kg_ref, a_ref, e_ref, hist_ref, do_ref,
             du_ref, dw_ref, dqg_ref, dkg_ref, da_ref, de_ref, ds_ref):
        @pl.when(pl.program_id(0) == 0)
        def _():
            ds_ref[...] = jnp.zeros_like(ds_ref)

        row0 = lax.broadcasted_iota(jnp.int32, (64, hd), 0) == 0

        def chunk(i, _):
            ci = nck - 1 - i
            rows = pl.ds(pl.multiple_of(ci * 64, 64), 64)
            for h in range(nh):
                s = hist_ref[ci, h]
                ds = ds_ref[h]
                dov = do_ref[h, rows, :]
                wv, kgv, av = w_ref[h, rows, :], kg_ref[h, rows, :], a_ref[h, rows, :]
                egl_row = e_ref[h, rows, :][0:1, :]
                v_new = u_ref[h, rows, :] - _dot16(wv, s)
                dv_new = _dot16(av, dov, "tn") + _dot16(kgv, ds)
                du_ref[h, rows, :] = dv_new
                dw_ref[h, rows, :] = -_dot16(dv_new, s, "nt")
                dqg_ref[h, rows, :] = _dot16(dov, s, "nt")
                dkg_ref[h, rows, :] = _dot16(v_new, ds, "nt")
                da_ref[h, rows, :] = _dot16(dov, v_new, "nt")
                de_ref[h, rows, :] = jnp.where(row0, jnp.sum(s * ds, axis=0, keepdims=True), 0.0)
                ds_ref[h] = ds * egl_row + _dot16(qg_ref[h, rows, :], dov, "tn") - _dot16(wv, dv_new, "tn")
            return 0

        lax.fori_loop(0, nck, chunk, 0)

    blk = pl.BlockSpec((nh, tb, hd), lambda i: (0, nblk - 1 - i, 0))
    hblk = pl.BlockSpec((nck, nh, hd, hd), lambda i: (nblk - 1 - i, 0, 0, 0))
    return pl.pallas_call(
        body, grid=(nblk,), in_specs=[blk] * 6 + [hblk, blk], out_specs=[blk] * 6,
        out_shape=[SDS((nh, ln, hd), f32)] * 6, scratch_shapes=[pltpu.VMEM((nh, hd, hd), f32)],
        name="dn_scan_bwd", compiler_params=_cparams())(u, w, qg, kg, attn, egl, hist, do)


@jax.custom_vjp
def dn_scan(u, w, qg, kg, attn, egl):
    return _dn_scan_fwd_call(u, w, qg, kg, attn, egl)[0]


def _dn_scan_fwd(u, w, qg, kg, attn, egl):
    o, hist = _dn_scan_fwd_call(u, w, qg, kg, attn, egl)
    return o, (u, w, qg, kg, attn, egl, hist)


def _dn_scan_bwd(res, do):
    return tuple(_dn_scan_bwd_call(*res, do))


dn_scan.defvjp(_dn_scan_fwd, _dn_scan_bwd)


def _cmul(a, b):
    return a[0] * b[0] - a[1] * b[1], a[0] * b[1] + a[1] * b[0]


def _powers(a1):
    a2 = _cmul(a1, a1)
    a3 = _cmul(a2, a1)
    a4 = _cmul(a2, a2)
    return [a1, a2, a3, a4, _cmul(a4, a1), _cmul(a4, a2), _cmul(a4, a3), _cmul(a4, a4)]


def _row_table(pw, row, order):
    tr = jnp.zeros(row.shape, f32)
    ti = jnp.zeros(row.shape, f32)
    for r in range(8):
        p = pw[order(r)]
        tr = tr + jnp.where(row == r, p[0], 0.0)
        ti = ti + jnp.where(row == r, p[1], 0.0)
    return tr, ti


def _s5_fwd_call(u, bmat, cmat, are, aim):
    ln, wu = u.shape
    w2 = bmat.shape[1]
    nl = w2 // 2
    tm = _pick(ln, (256, 128))

    def body(u_ref, b_ref, c_ref, are_ref, aim_ref, st_ref, y_ref, bu_ref, cr_ref, ci_ref):
        @pl.when(pl.program_id(0) == 0)
        def _():
            cr_ref[...] = jnp.zeros_like(cr_ref)
            ci_ref[...] = jnp.zeros_like(ci_ref)

        bu_ref[...] = _dot16(u_ref[...], b_ref[...])
        pw = _powers((are_ref[...], aim_ref[...]))
        row = lax.broadcasted_iota(jnp.int32, (8, nl), 0)
        table = _row_table(pw, row, lambda r: r)

        def tile(t, c):
            rows = pl.ds(pl.multiple_of(t * 8, 8), 8)
            x = (bu_ref[rows, 0:nl], bu_ref[rows, nl:w2])
            for kk in (1, 2, 4):
                sh = (jnp.where(row >= kk, pltpu.roll(x[0], kk, 0), 0.0), jnp.where(row >= kk, pltpu.roll(x[1], kk, 0), 0.0))
                m = _cmul(pw[kk - 1], sh)
                x = (x[0] + m[0], x[1] + m[1])
            m = _cmul(table, c)
            x = (x[0] + m[0], x[1] + m[1])
            st_ref[rows, 0:nl] = x[0]
            st_ref[rows, nl:w2] = x[1]
            return (jnp.sum(jnp.where(row == 7, x[0], 0.0), axis=0, keepdims=True),
                    jnp.sum(jnp.where(row == 7, x[1], 0.0), axis=0, keepdims=True))

        c = lax.fori_loop(0, tm // 8, tile, (cr_ref[...], ci_ref[...]))
        cr_ref[...] = c[0]
        ci_ref[...] = c[1]
        y_ref[...] = _dot16(st_ref[...], c_ref[...])

    vec = pl.BlockSpec((1, nl), lambda i: (0, 0))
    whole = lambda a: pl.BlockSpec(a.shape, lambda i: (0, 0))
    rows = lambda c: pl.BlockSpec((tm, c), lambda i: (i, 0))
    return pl.pallas_call(
        body, grid=(ln // tm,), in_specs=[rows(wu), whole(bmat), whole(cmat), vec, vec], out_specs=[rows(w2), rows(wu)],
        out_shape=[SDS((ln, w2), f32), SDS((ln, wu), f32)],
        scratch_shapes=[pltpu.VMEM((tm, w2), f32), pltpu.VMEM((1, nl), f32), pltpu.VMEM((1, nl), f32)],
        name="s5_fwd", compiler_params=_cparams())(u, bmat, cmat, are, aim)


def _s5_bwd_call(u, bmat, cmat, are, aim, st, dy):
    ln, wu = u.shape
    w2 = bmat.shape[1]
    nl = w2 // 2
    tm = _pick(ln, (256, 128))
    n = ln // tm
    t8 = tm // 8

    def body(u_ref, st_ref, halo_ref, dy_ref, b_hbm, c_hbm, are_ref, aim_ref, du_ref, db_hbm, dc_hbm, dar_ref, dai_ref,
             b_ref, c_ref, db_ref, dc_ref, sb_ref, d_ref, cr_ref, ci_ref):
        i = pl.program_id(0)
        g_ref = d_ref

        @pl.when(i == 0)
        def _():
            pltpu.sync_copy(b_hbm, b_ref)
            pltpu.sync_copy(c_hbm, c_ref)
            db_ref[...] = jnp.zeros_like(db_ref)
            dc_ref[...] = jnp.zeros_like(dc_ref)
            cr_ref[...] = jnp.zeros_like(cr_ref)
            ci_ref[...] = jnp.zeros_like(ci_ref)
            dar_ref[...] = jnp.zeros_like(dar_ref)
            dai_ref[...] = jnp.zeros_like(dai_ref)

        d_ref[...] = _dot16(dy_ref[...], c_ref[...], "nt")
        sb_ref[0:8, :] = jnp.where(i < n - 1, halo_ref[...], 0.0)
        sb_ref[8:tm + 8, :] = st_ref[...]
        pw = _powers((are_ref[...], -aim_ref[...]))
        row = lax.broadcasted_iota(jnp.int32, (8, nl), 0)
        table = _row_table(pw, row, lambda r: 7 - r)

        def tile(j, carry):
            c, acc_r, acc_i = carry
            t = t8 - 1 - j
            rows = pl.ds(pl.multiple_of(t * 8, 8), 8)
            x = (d_ref[rows, 0:nl], d_ref[rows, nl:w2])
            for kk in (1, 2, 4):
                sh = (jnp.where(row < 8 - kk, pltpu.roll(x[0], 8 - kk, 0), 0.0),
                      jnp.where(row < 8 - kk, pltpu.roll(x[1], 8 - kk, 0), 0.0))
                m = _cmul(pw[kk - 1], sh)
                x = (x[0] + m[0], x[1] + m[1])
            m = _cmul(table, c)
            x = (x[0] + m[0], x[1] + m[1])
            g_ref[rows, 0:nl] = x[0]
            g_ref[rows, nl:w2] = x[1]
            prev = (sb_ref[rows, 0:nl], sb_ref[rows, nl:w2])
            cur = (st_ref[rows, 0:nl], st_ref[rows, nl:w2])
            last_r = jnp.sum(jnp.where(row == 7, prev[0], 0.0), axis=0, keepdims=True)
            last_i = jnp.sum(jnp.where(row == 7, prev[1], 0.0), axis=0, keepdims=True)
            sp = (jnp.where(row == 0, last_r, pltpu.roll(cur[0], 1, 0)), jnp.where(row == 0, last_i, pltpu.roll(cur[1], 1, 0)))
            acc_r = acc_r + x[0] * sp[0] + x[1] * sp[1]
            acc_i = acc_i + x[1] * sp[0] - x[0] * sp[1]
            c = (jnp.sum(jnp.where(row == 0, x[0], 0.0), axis=0, keepdims=True),
                 jnp.sum(jnp.where(row == 0, x[1], 0.0), axis=0, keepdims=True))
            return c, acc_r, acc_i

        z8 = jnp.zeros((8, nl), f32)
        c, acc_r, acc_i = lax.fori_loop(0, t8, tile, ((cr_ref[...], ci_ref[...]), z8, z8))
        cr_ref[...] = c[0]
        ci_ref[...] = c[1]
        dar_ref[...] += jnp.sum(acc_r, axis=0, keepdims=True)
        dai_ref[...] += jnp.sum(acc_i, axis=0, keepdims=True)
        g = g_ref[...].astype(bf16)
        du_ref[...] = lax.dot_general(g, b_ref[...], _DIMS["nt"], preferred_element_type=f32)
        db_ref[...] += lax.dot_general(u_ref[...].astype(bf16), g, _DIMS["tn"], preferred_element_type=f32)
        dc_ref[...] += _dot16(st_ref[...], dy_ref[...], "tn")

        @pl.when(i == n - 1)
        def _():
            pltpu.sync_copy(db_ref, db_hbm)
            pltpu.sync_copy(dc_ref, dc_hbm)

    vec = pl.BlockSpec((1, nl), lambda i: (0, 0))
    rows = lambda c: pl.BlockSpec((tm, c), lambda i: (n - 1 - i, 0))
    halo = pl.BlockSpec((8, w2), lambda i: (jnp.maximum((n - 1 - i) * t8 - 1, 0), 0))
    anyspec = pl.BlockSpec(memory_space=pl.ANY)
    return pl.pallas_call(
        body, grid=(n,), in_specs=[rows(wu), rows(w2), halo, rows(wu), anyspec, anyspec, vec, vec],
        out_specs=[rows(wu), anyspec, anyspec, vec, vec],
        out_shape=[SDS((ln, wu), f32), SDS(bmat.shape, f32), SDS(cmat.shape, f32), SDS((1, nl), f32), SDS((1, nl), f32)],
        scratch_shapes=[pltpu.VMEM(bmat.shape, bf16), pltpu.VMEM(cmat.shape, bf16), pltpu.VMEM(bmat.shape, f32),
                        pltpu.VMEM(cmat.shape, f32), pltpu.VMEM((tm + 8, w2), f32), pltpu.VMEM((tm, w2), f32),
                        pltpu.VMEM((1, nl), f32), pltpu.VMEM((1, nl), f32)],
        name="s5_bwd", compiler_params=_cparams())(u, st, st, dy, bmat, cmat, are, aim)


@jax.custom_vjp
def s5_core(u, bmat, cmat, are, aim):
    return _s5_fwd_call(u, bmat.astype(bf16), cmat.astype(bf16), are, aim)[1]


def _s5_core_fwd(u, bmat, cmat, are, aim):
    bb, cb = bmat.astype(bf16), cmat.astype(bf16)
    st, y = _s5_fwd_call(u, bb, cb, are, aim)
    return y, (u, bb, cb, are, aim, st)


def _s5_core_bwd(res, dy):
    return tuple(_s5_bwd_call(*res, dy))


s5_core.defvjp(_s5_core_fwd, _s5_core_bwd)


def _final_call(h, target, gnorm):
    ln, d = h.shape
    tm = PAD + N_META
    skip = (ln - target.shape[0]) // tm

    def body(h_ref, t_ref, g_ref, loss_ref, dh_ref, dg_ref):
        i = pl.program_id(0)

        @pl.when(i == 0)
        def _():
            loss_ref[...] = jnp.zeros_like(loss_ref)
            dg_ref[...] = jnp.zeros_like(dg_ref)

        live = (i >= skip).astype(f32)
        y, vjp = jax.vjp(_rms, h_ref[...], g_ref[...])
        err = (y - t_ref[...]) * live
        loss_ref[...] += 0.5 * jnp.sum(jnp.sum(err * err, axis=1, keepdims=True) / d, axis=0, keepdims=True)
        dh, dg = vjp(err / d)
        dh_ref[...] = dh
        dg_ref[...] += dg

    return pl.pallas_call(
        body, grid=(ln // tm,),
        in_specs=[pl.BlockSpec((tm, d), lambda i: (i, 0)), pl.BlockSpec((tm, d), lambda i: (jnp.maximum(i - skip, 0), 0)),
                  pl.BlockSpec((1, d), lambda i: (0, 0))],
        out_specs=[pl.BlockSpec((1, 1), lambda i: (0, 0)), pl.BlockSpec((tm, d), lambda i: (i, 0)), pl.BlockSpec((1, d), lambda i: (0, 0))],
        out_shape=[SDS((1, 1), f32), SDS((ln, d), f32), SDS((1, d), f32)], name="final_loss", compiler_params=_cparams())(h, target, gnorm)


def _exchange(x, gather, name):
    r, c = x.shape[-2:]

    def body(x_ref, o_ref, send_sems, recv_sems, local_sem):
        ax, ay, ac = lax.axis_index("x"), lax.axis_index("y"), lax.axis_index("c")
        me = 4 * ax + 2 * ay + ac
        local = pltpu.make_async_copy(x_ref if gather else x_ref.at[me], o_ref.at[me], local_sem)
        local.start()
        sent = []
        for k in range(1, N_DEV):
            px = 1 - ax if k & 4 else ax
            py = 1 - ay if k & 2 else ay
            pc = 1 - ac if k & 1 else ac
            p = 4 * px + 2 * py + pc
            cp = pltpu.make_async_remote_copy(
                src_ref=x_ref if gather else x_ref.at[p], dst_ref=o_ref.at[me], send_sem=send_sems.at[k - 1],
                recv_sem=recv_sems.at[k - 1], device_id=(px, py, pc), device_id_type=pl.DeviceIdType.MESH)
            cp.start()
            sent.append((cp, p, (px, py, pc)))
        for k, (cp, p, peer) in enumerate(sent):
            pltpu.make_async_remote_copy(
                src_ref=o_ref.at[p], dst_ref=o_ref.at[p], send_sem=send_sems.at[k], recv_sem=recv_sems.at[k],
                device_id=peer, device_id_type=pl.DeviceIdType.MESH).wait_recv()
        for cp, _, _ in sent:
            cp.wait_send()
        local.wait()

    hbm = pl.BlockSpec(memory_space=pltpu.HBM)
    return pl.pallas_call(
        body, in_specs=[hbm], out_specs=hbm, out_shape=SDS((N_DEV, r, c), x.dtype),
        scratch_shapes=[pltpu.SemaphoreType.DMA((N_DEV - 1,)), pltpu.SemaphoreType.DMA((N_DEV - 1,)), pltpu.SemaphoreType.DMA],
        name=name)(x)


def _gather_via_sibling(x, name):
    r, c = x.shape

    def body(x_ref, o_ref, send_sems, recv_sems, local_sem):
        ax, ay, ac = lax.axis_index("x"), lax.axis_index("y"), lax.axis_index("c")
        me, sibling = (ax, ay, ac), (ax, ay, 1 - ac)
        chips = [(1 - ax, ay), (ax, 1 - ay), (1 - ax, 1 - ay)]

        def slot(px, py, pc):
            return o_ref.at[4 * px + 2 * py + pc]

        def copy(k, block, to, src=None):
            return pltpu.make_async_remote_copy(
                src_ref=slot(*block) if src is None else src, dst_ref=slot(*block), send_sem=send_sems.at[k],
                recv_sem=recv_sems.at[k], device_id=to, device_id_type=pl.DeviceIdType.MESH)

        mine = pltpu.make_async_copy(x_ref, slot(*me), local_sem)
        mine.start()
        first = [copy(0, me, sibling, src=x_ref)] + [copy(1 + j, me, (*chip, ac), src=x_ref) for j, chip in enumerate(chips)]
        for cp in first:
            cp.start()
        passed = [copy(4 + j, (*chip, ac), sibling) for j, chip in enumerate(chips)]
        for j, chip in enumerate(chips):
            copy(1 + j, (*chip, ac), me).wait_recv()
            passed[j].start()
        copy(0, sibling, me).wait_recv()
        for j, chip in enumerate(chips):
            copy(4 + j, (*chip, 1 - ac), me).wait_recv()
        for cp in first + passed:
            cp.wait_send()
        mine.wait()

    hbm = pl.BlockSpec(memory_space=pltpu.HBM)
    return pl.pallas_call(
        body, in_specs=[hbm], out_specs=hbm, out_shape=SDS((N_DEV, r, c), x.dtype),
        scratch_shapes=[pltpu.SemaphoreType.DMA((N_DEV - 1,)), pltpu.SemaphoreType.DMA((N_DEV - 1,)), pltpu.SemaphoreType.DMA],
        name=name)(x)


def _sum_slots(x, name):
    _, r, c = x.shape
    tr = _pick(r, (PACK_BLOCK_ROWS, 256, 128, 64, 32, 16, 8))

    def body(x_ref, o_ref):
        acc = x_ref[0].astype(f32)
        for p in range(1, N_DEV):
            acc = acc + x_ref[p].astype(f32)
        o_ref[...] = acc

    return pl.pallas_call(body, grid=(r // tr,), in_specs=[pl.BlockSpec((N_DEV, tr, c), lambda i: (0, i, 0))],
                          out_specs=pl.BlockSpec((tr, c), lambda i: (i, 0)), out_shape=SDS((r, c), f32), name=name,
                          compiler_params=_cparams())(x)


def _adamw(w, g, m, v, name):
    r, c = w.shape
    tr = _pick(r, (PACK_BLOCK_ROWS, 256, 128, 64, 32, 16, 8))

    def body(w_ref, g_ref, m_ref, v_ref, d_ref, mo_ref, vo_ref):
        gv = g_ref[...]
        mn = ADAM_B1 * m_ref[...] + (1.0 - ADAM_B1) * gv
        vn = ADAM_B2 * v_ref[...] + (1.0 - ADAM_B2) * jnp.square(gv)
        m_hat = mn / (1.0 - ADAM_B1 ** ADAM_STEP)
        v_hat = vn / (1.0 - ADAM_B2 ** ADAM_STEP)
        d_ref[...] = -ADAM_LR * (m_hat / (jnp.sqrt(v_hat) + ADAM_EPS) + ADAM_WD * w_ref[...])
        mo_ref[...] = mn
        vo_ref[...] = vn

    blk = pl.BlockSpec((tr, c), lambda i: (i, 0))
    return pl.pallas_call(body, grid=(r // tr,), in_specs=[blk] * 4, out_specs=[blk] * 3, out_shape=[SDS((r, c), f32)] * 3,
                          name=name, compiler_params=_cparams())(w, g, m, v)


_WEIGHTS = [
    ("meta_tokens", (N_META, D_MODEL), 1), ("ffn1_norm", (DEPTH, D_MODEL), None),
    ("ffn1_w_gate", (DEPTH, D_MODEL, D_FF), 2), ("ffn1_w_up", (DEPTH, D_MODEL, D_FF), 2),
    ("ffn1_w_down", (DEPTH, D_FF, D_MODEL), 1), ("mix_norm", (DEPTH, D_MODEL), None),
    ("w_in", (DEPTH, D_MODEL, IN_WIDTH), 2), ("sb_out_norm", (DEPTH, HEAD_DIM), None),
    ("dn_conv_w", (DEPTH, DN_CONV, 3 * GW), 2), ("dn_a_log", (DEPTH, N_HEADS), None),
    ("dn_dt_bias", (DEPTH, N_HEADS), None), ("dn_out_norm", (DEPTH, HEAD_DIM), None),
    ("s5_a_re", (DEPTH, S5_GROUPS, S5_STATE), None), ("s5_a_im", (DEPTH, S5_GROUPS, S5_STATE), None),
    ("s5_log_dt", (DEPTH, S5_GROUPS), None), ("s5_b_re", (DEPTH, S5_GROUPS, S5_STATE, S5_GROUP), None),
    ("s5_b_im", (DEPTH, S5_GROUPS, S5_STATE, S5_GROUP), None), ("s5_c_re", (DEPTH, S5_GROUPS, S5_GROUP, S5_STATE), None),
    ("s5_c_im", (DEPTH, S5_GROUPS, S5_GROUP, S5_STATE), None), ("s5_d", (DEPTH, S5_WIDTH), None),
    ("s5_w_glu", (DEPTH, S5_WIDTH, S5_WIDTH), 1), ("s5_b_glu", (DEPTH, S5_WIDTH), None),
    ("s5_out_norm", (DEPTH, S5_WIDTH), None), ("w_out", (DEPTH, D_MODEL, D_MODEL), 1),
    ("ffn2_norm", (DEPTH, D_MODEL), None), ("ffn2_w_gate", (DEPTH, D_MODEL, D_FF), 2),
    ("ffn2_w_up", (DEPTH, D_MODEL, D_FF), 2), ("ffn2_w_down", (DEPTH, D_FF, D_MODEL), 1),
    ("final_norm", (D_MODEL,), None),
]
_SHARDED = [(n, s, a) for n, s, a in _WEIGHTS if a is not None]
_REPL = [(n, s) for n, s, a in _WEIGHTS if a is None]
PACK_ROW_ALIGN = 16
PACK_BLOCK_ROWS = 256


def _shard_shape(shape, axis):
    return tuple(d // N_DEV if i == axis else d for i, d in enumerate(shape))


def _pack_rows_of(n):
    rows = -(-n // PACK_COLS)
    return -(-rows // PACK_ROW_ALIGN) * PACK_ROW_ALIGN


def _as_rows(t, lead):
    head = t.shape[:lead]
    n = math.prod(t.shape[lead:])
    rows = _pack_rows_of(n)
    if n % PACK_COLS == 0:
        t = t.reshape(head + (n // PACK_COLS, PACK_COLS))
        return jnp.pad(t, [(0, 0)] * lead + [(0, rows - n // PACK_COLS), (0, 0)])
    t = jnp.pad(t.reshape(head + (n,)), [(0, 0)] * lead + [(0, rows * PACK_COLS - n)])
    return t.reshape(head + (rows, PACK_COLS))


def _pack(parts, lead=0):
    rows = [_as_rows(p, lead) for p in parts]
    total = sum(r.shape[lead] for r in rows)
    fill = -total % PACK_BLOCK_ROWS
    if fill:
        rows.append(jnp.zeros(rows[0].shape[:lead] + (fill, PACK_COLS), rows[0].dtype))
    return jnp.concatenate(rows, axis=lead)


def _unpack(pack, shapes, lead=0):
    out, off = [], 0
    head = pack.shape[:lead]
    for s in shapes:
        n = math.prod(s)
        rows = _pack_rows_of(n)
        blk = lax.slice_in_dim(pack, off, off + rows, axis=lead)
        if n % PACK_COLS == 0:
            out.append(lax.slice_in_dim(blk, 0, n // PACK_COLS, axis=lead).reshape(head + tuple(s)))
        else:
            out.append(blk.reshape(head + (rows * PACK_COLS,))[..., :n].reshape(head + tuple(s)))
        off += rows
    return out


def _unpack_gathered(g):
    blocks = _unpack(g, [_shard_shape(s, a) for _, s, a in _SHARDED], lead=1)
    return {name: jnp.moveaxis(blk.astype(f32), 0, axis).reshape(shape)
            for (name, shape, axis), blk in zip(_SHARDED, blocks)}


def _pack_by_dest(grads):
    parts = []
    for name, shape, axis in _SHARDED:
        ss = _shard_shape(shape, axis)
        g = grads[name].reshape(shape[:axis] + (N_DEV, ss[axis]) + shape[axis + 1:])
        parts.append(jnp.moveaxis(g, axis, 0))
    return _pack(parts, lead=1)


_rmsnorm_op = _make_blockop(_f_rmsnorm, "rmsnorm", (D_MODEL,), (640, 256, 128), (bf16,))
_swiglu_op = _make_blockop(_f_swiglu, "swiglu", (D_FF,), (256, 128), (bf16,))
_headnorm_op = _make_blockop(_f_headnorm, "sb_headnorm", (GW,), (640, 256, 128))
_dn_prep_op = _make_blockop(_f_dn_prep, "dn_prep", (GW,) * 5, (128,))
_dn_intra_op = _make_blockop(_f_dn_intra_heads, "dn_intra", (HEAD_DIM,) * 6, (256,), whole_lead=True)
_dn_out_op = _make_blockop(_f_dn_out, "dn_out", (GW,), (640, 256, 128))
_s5_param_op = _make_blockop(_f_s5_param, "s5_param", (1, 1, S5_GROUP, S5_GROUP), (256,))
_s5_post_op = _make_blockop(_f_s5_post, "s5_post", (S5_WIDTH,), (256, 128))
_lin_gu, _ = _make_linear("ffn_gu", out_dtype=bf16)
_, _lin_down_res = _make_linear("ffn_down", scale=0.5)
_lin_in, _ = _make_linear("mix_in")
_, _lin_out_res = _make_linear("mix_out")


def _heads(t):
    return jnp.transpose(t.reshape(t.shape[0], N_HEADS, HEAD_DIM), (1, 0, 2))


def _unheads(t):
    return jnp.transpose(t, (1, 0, 2)).reshape(t.shape[1], GW)


def _ffn(h, gnorm, wg, wu, wd):
    (xn,) = _rmsnorm_op((h,), (gnorm[None],))
    gu = _lin_gu(xn, jnp.concatenate([wg, wu], axis=1))
    (a,) = _swiglu_op((gu,), ())
    return _lin_down_res(a, wd, h)


def _lane_row(vals, start):
    return jnp.pad(vals, (start, 128 - start - vals.shape[0]))[None]


def _block_diag(t):
    g, a, b = t.shape
    eye = jnp.eye(g, dtype=t.dtype)
    return (t[:, :, None, :] * eye[:, None, :, None]).reshape(g * a, g * b)


def _mixer(h, p):
    (xn,) = _rmsnorm_op((h,), (p["mix_norm"][None],))
    w_in = p["w_in"]
    w_pad = jnp.concatenate([w_in[:, :IN_SMALL], jnp.zeros((D_MODEL, C_S5U - IN_SMALL), f32), w_in[:, IN_SMALL:],
                             jnp.zeros((D_MODEL, IN_PAD - C_S5U - S5_WIDTH), f32)], axis=1)
    proj = _lin_in(xn, w_pad)
    o_sb = sb_attention(_heads(proj[:, C_SBQ:C_SBK]), _heads(proj[:, C_SBK:C_SBV]), _heads(proj[:, C_SBV:C_DNQKV]))
    (o_sb,) = _headnorm_op((_unheads(o_sb),), (jnp.tile(p["sb_out_norm"], N_HEADS)[None],))
    conv = conv_op(proj[:, C_DNQKV:C_DNZ], jnp.pad(p["dn_conv_w"], ((0, 8 - DN_CONV), (0, 0))))
    q, k, v, gc, bb = _dn_prep_op((conv, proj[:, C_DNBA:C_S5U]),
                                  (_lane_row(p["dn_a_log"], N_HEADS), _lane_row(p["dn_dt_bias"], N_HEADS)))
    parts = _dn_intra_op(tuple(_heads(t) for t in (q, k, v, gc, bb)), ())
    o_dn = _unheads(dn_scan(*parts))
    (o_dn,) = _dn_out_op((o_dn, proj[:, C_DNZ:C_DNBA]), (jnp.tile(p["dn_out_norm"], N_HEADS)[None],))
    u = proj[:, C_S5U:C_S5U + S5_WIDTH]
    col = lambda t: t.reshape(S5_LANES, 1)
    abr, abi, bbr, bbi = _s5_param_op(
        (col(p["s5_a_re"]), col(p["s5_a_im"]), col(jnp.repeat(p["s5_log_dt"], S5_STATE)),
         p["s5_b_re"].reshape(S5_LANES, S5_GROUP), p["s5_b_im"].reshape(S5_LANES, S5_GROUP)), ())
    to_b = lambda t: _block_diag(jnp.transpose(t.reshape(S5_GROUPS, S5_STATE, S5_GROUP), (0, 2, 1)))
    to_c = lambda t: _block_diag(jnp.transpose(t, (0, 2, 1)))
    y = s5_core(u, jnp.concatenate([to_b(bbr), to_b(bbi)], axis=1),
                jnp.concatenate([to_c(p["s5_c_re"]), -to_c(p["s5_c_im"])], axis=0),
                abr.reshape(1, S5_LANES), abi.reshape(1, S5_LANES))
    (o_s5,) = _s5_post_op((y, u), (p["s5_d"][None], p["s5_w_glu"], p["s5_b_glu"][None], p["s5_out_norm"][None]))
    return _lin_out_res(jnp.concatenate([o_sb, o_dn, o_s5], axis=1), p["w_out"], h)


def _trunk(x2d, w):
    h = jnp.concatenate([jnp.zeros((PAD, D_MODEL), f32), w["meta_tokens"], x2d], axis=0)
    for l in range(DEPTH):
        p = {k: v[l] for k, v in w.items() if k not in ("meta_tokens", "final_norm")}
        h = _ffn(h, p["ffn1_norm"], p["ffn1_w_gate"], p["ffn1_w_up"], p["ffn1_w_down"])
        h = _mixer(h, p)
        h = _ffn(h, p["ffn2_norm"], p["ffn2_w_gate"], p["ffn2_w_up"], p["ffn2_w_down"])
    return h


def _step(x, loss_target, w, m, v):
    s_names = [n for n, _, _ in _SHARDED]
    r_names = [n for n, _ in _REPL]
    s_shapes = [_shard_shape(s, a) for _, s, a in _SHARDED]
    r_shapes = [s for _, s in _REPL] + [(1,)]
    shard_pack = _pack([w[n] for n in s_names])
    gathered = _gather_via_sibling(shard_pack.astype(bf16), "gather_weights")
    full = _unpack_gathered(gathered)
    full.update({n: w[n] for n, _ in _REPL})
    trunk_w = {k: t for k, t in full.items() if k != "final_norm"}
    h, vjp = jax.vjp(_trunk, x[0], trunk_w)
    loss, dh, dgf = _final_call(h, loss_target[0], full["final_norm"][None])
    dx, dw = vjp(dh)
    dw["final_norm"] = dgf[0]
    g_shard = _sum_slots(_exchange(_pack_by_dest(dw).astype(bf16), False, "scatter_grads"), "sum_shard_grads")
    repl_pack = _pack([dw[n] for n in r_names] + [loss.reshape(1)])
    g_repl = _sum_slots(_exchange(repl_pack, True, "gather_small_grads"), "sum_small_grads")
    outs = {}
    zero1 = jnp.zeros((1,), f32)
    small = lambda d: _pack([d[n] for n in r_names] + [zero1])
    d_s, m_s, v_s = _adamw(shard_pack, g_shard, _pack([m[n] for n in s_names]), _pack([v[n] for n in s_names]), "adamw_shards")
    d_r, m_r, v_r = _adamw(small(w), g_repl, small(m), small(v), "adamw_small")
    for kind, fs, fr in (("grad", g_shard, g_repl), ("delta", d_s, d_r), ("new_m", m_s, m_r), ("new_v", v_s, v_r)):
        for n, t in zip(s_names, _unpack(fs, s_shapes)):
            outs[kind + "_" + n] = t
        for n, t in zip(r_names + ["loss"], _unpack(fr, r_shapes)):
            outs[kind + "_" + n] = t
    loss_total = outs["grad_loss"][0]
    names = [n for n, _, _ in _WEIGHTS]
    return (loss_total, dx[None], *[outs["grad_" + n] for n in names], *[outs["delta_" + n] for n in names],
            *[outs["new_m_" + n] for n in names], *[outs["new_v_" + n] for n in names])


def kernel(x, meta_tokens, ffn1_norm, ffn1_w_gate, ffn1_w_up, ffn1_w_down, mix_norm, w_in, sb_out_norm, dn_conv_w, dn_a_log, dn_dt_bias, dn_out_norm, s5_a_re, s5_a_im, s5_log_dt, s5_b_re, s5_b_im, s5_c_re, s5_c_im, s5_d, s5_w_glu, s5_b_glu, s5_out_norm, w_out, ffn2_norm, ffn2_w_gate, ffn2_w_up, ffn2_w_down, final_norm, loss_target, m_meta_tokens, m_ffn1_norm, m_ffn1_w_gate, m_ffn1_w_up, m_ffn1_w_down, m_mix_norm, m_w_in, m_sb_out_norm, m_dn_conv_w, m_dn_a_log, m_dn_dt_bias, m_dn_out_norm, m_s5_a_re, m_s5_a_im, m_s5_log_dt, m_s5_b_re, m_s5_b_im, m_s5_c_re, m_s5_c_im, m_s5_d, m_s5_w_glu, m_s5_b_glu, m_s5_out_norm, m_w_out, m_ffn2_norm, m_ffn2_w_gate, m_ffn2_w_up, m_ffn2_w_down, m_final_norm, v_meta_tokens, v_ffn1_norm, v_ffn1_w_gate, v_ffn1_w_up, v_ffn1_w_down, v_mix_norm, v_w_in, v_sb_out_norm, v_dn_conv_w, v_dn_a_log, v_dn_dt_bias, v_dn_out_norm, v_s5_a_re, v_s5_a_im, v_s5_log_dt, v_s5_b_re, v_s5_b_im, v_s5_c_re, v_s5_c_im, v_s5_d, v_s5_w_glu, v_s5_b_glu, v_s5_out_norm, v_w_out, v_ffn2_norm, v_ffn2_w_gate, v_ffn2_w_up, v_ffn2_w_down, v_final_norm):
    names = [n for n, _, _ in _WEIGHTS]
    ws = (meta_tokens, ffn1_norm, ffn1_w_gate, ffn1_w_up, ffn1_w_down, mix_norm, w_in, sb_out_norm, dn_conv_w, dn_a_log, dn_dt_bias, dn_out_norm, s5_a_re, s5_a_im, s5_log_dt, s5_b_re, s5_b_im, s5_c_re, s5_c_im, s5_d, s5_w_glu, s5_b_glu, s5_out_norm, w_out, ffn2_norm, ffn2_w_gate, ffn2_w_up, ffn2_w_down, final_norm)
    ms = (m_meta_tokens, m_ffn1_norm, m_ffn1_w_gate, m_ffn1_w_up, m_ffn1_w_down, m_mix_norm, m_w_in, m_sb_out_norm, m_dn_conv_w, m_dn_a_log, m_dn_dt_bias, m_dn_out_norm, m_s5_a_re, m_s5_a_im, m_s5_log_dt, m_s5_b_re, m_s5_b_im, m_s5_c_re, m_s5_c_im, m_s5_d, m_s5_w_glu, m_s5_b_glu, m_s5_out_norm, m_w_out, m_ffn2_norm, m_ffn2_w_gate, m_ffn2_w_up, m_ffn2_w_down, m_final_norm)
    vs = (v_meta_tokens, v_ffn1_norm, v_ffn1_w_gate, v_ffn1_w_up, v_ffn1_w_down, v_mix_norm, v_w_in, v_sb_out_norm, v_dn_conv_w, v_dn_a_log, v_dn_dt_bias, v_dn_out_norm, v_s5_a_re, v_s5_a_im, v_s5_log_dt, v_s5_b_re, v_s5_b_im, v_s5_c_re, v_s5_c_im, v_s5_d, v_s5_w_glu, v_s5_b_glu, v_s5_out_norm, v_w_out, v_ffn2_norm, v_ffn2_w_gate, v_ffn2_w_up, v_ffn2_w_down, v_final_norm)
    return _step(x, loss_target, dict(zip(names, ws)), dict(zip(names, ms)), dict(zip(names, vs)))
```

```python
import functools
import math

import jax
import jax.numpy as jnp
from jax import lax
from jax.experimental import pallas as pl
from jax.experimental.pallas import tpu as pltpu

f32 = jnp.float32
bf16 = jnp.bfloat16
HI = lax.Precision.HIGHEST
SDS = jax.ShapeDtypeStruct

N_DEV = 8
D_MODEL = 1024
N_META = 16
PAD = 240
HEAD_DIM = 64
N_HEADS = 4
GW = N_HEADS * HEAD_DIM
DN_CONV = 4
S5_WIDTH = 512
S5_GROUP = 16
S5_GROUPS = 32
S5_STATE = 64
S5_LANES = S5_GROUPS * S5_STATE
D_FF = 2816
DEPTH = 2
EPS = 1e-6
C_SBQ, C_SBK, C_SBV, C_DNQKV, C_DNZ, C_DNBA, C_S5U, IN_PAD = 0, 256, 512, 768, 1536, 1792, 1920, 2560
IN_WIDTH = 2312
IN_SMALL = 1800
VMEM_LIMIT = 56 * 1024 * 1024
PACK_COLS = 512

ADAM_LR, ADAM_B1, ADAM_B2, ADAM_EPS, ADAM_WD, ADAM_STEP = 0.001, 0.9, 0.999, 1e-08, 0.01, 10


def _pick(n, cands):
    for c in cands:
        if n % c == 0:
            return c
    return n


def _cparams():
    return pltpu.CompilerParams(vmem_limit_bytes=VMEM_LIMIT)


_DIMS = {"nn": (((1,), (0,)), ((), ())), "nt": (((1,), (1,)), ((), ())), "tn": (((0,), (0,)), ((), ()))}


def _dot16(a, b, mode="nn"):
    return lax.dot_general(a.astype(bf16), b.astype(bf16), _DIMS[mode], preferred_element_type=f32)


def _make_bdot(mode):
    @jax.custom_vjp
    def f(a, b):
        return _dot16(a, b, mode)

    def fwd(a, b):
        return _dot16(a, b, mode), (a, b)

    def bwd(res, g):
        a, b = res
        if mode == "nn":
            return _dot16(g, b, "nt"), _dot16(a, g, "tn")
        if mode == "nt":
            return _dot16(g, b, "nn"), _dot16(g, a, "tn")
        return _dot16(b, g, "nt"), _dot16(a, g, "nn")

    f.defvjp(fwd, bwd)
    return f


bdot = _make_bdot("nn")
bdot_nt = _make_bdot("nt")
bdot_tn = _make_bdot("tn")


def _split3(x):
    h = x.astype(bf16)
    r = x - h.astype(f32)
    m = r.astype(bf16)
    l = (r - m.astype(f32)).astype(bf16)
    return h, m, l


def _dot3(a, b, mode="nn"):
    ah, am, al = _split3(a)
    bh, bm, bl = _split3(b)
    d = lambda x, y: lax.dot_general(x, y, _DIMS[mode], preferred_element_type=f32)
    return ((d(al, bh) + d(ah, bl)) + d(am, bm)) + ((d(am, bh) + d(ah, bm)) + d(ah, bh))


def _make_xdot(dot):
    @jax.custom_vjp
    def f(a, b):
        return dot(a, b, "nn")

    def fwd(a, b):
        return dot(a, b, "nn"), (a, b)

    def bwd(res, g):
        a, b = res
        return dot(g, b, "nt"), dot(a, g, "tn")

    f.defvjp(fwd, bwd)
    return f


xdot = _make_xdot(_dot3)


def _sel_dot(sel, x, mode):
    s = sel.astype(bf16)
    h, m, l = _split3(x)
    d = lambda y: lax.dot_general(s, y, _DIMS[mode], preferred_element_type=f32)
    return (d(l) + d(m)) + d(h)


@jax.custom_vjp
def _select_rows(sel, x):
    return _sel_dot(sel, x, "nn")


def _select_rows_fwd(sel, x):
    return _sel_dot(sel, x, "nn"), sel


def _select_rows_bwd(sel, g):
    return jnp.zeros_like(sel), _sel_dot(sel, g, "tn")


_select_rows.defvjp(_select_rows_fwd, _select_rows_bwd)


def _matmul(a, b, mode, name, out_dtype=f32, scale=None, res=None):
    row_c = (1280, 640, 512, 384, 256, 128)
    col_c = (1408, 1280, 1024, 512, 256, 128)
    if mode == "nn":
        (m, k), n = a.shape, b.shape[1]
        bo1, bo2, br = _pick(m, row_c), _pick(n, col_c), _pick(k, (1024, 1408, 1280, 512, 256, 128))
        out, red = (m, n), k
        a_spec = pl.BlockSpec((bo1, br), lambda i, j, r: (i, r))
        b_spec = pl.BlockSpec((br, bo2), lambda i, j, r: (r, j))
    elif mode == "nt":
        (m, n), k = a.shape, b.shape[0]
        bo1, bo2, br = _pick(m, row_c), _pick(k, col_c), _pick(n, (1408, 1280, 1024, 512, 256, 128))
        out, red = (m, k), n
        a_spec = pl.BlockSpec((bo1, br), lambda i, j, r: (i, r))
        b_spec = pl.BlockSpec((bo2, br), lambda i, j, r: (j, r))
    else:
        (m, k), n = a.shape, b.shape[1]
        bo1, bo2, br = _pick(k, (1024, 1408, 1280, 512, 256, 128)), _pick(n, col_c), _pick(m, row_c[1:])
        out, red = (k, n), m
        a_spec = pl.BlockSpec((br, bo1), lambda i, j, r: (r, i))
        b_spec = pl.BlockSpec((br, bo2), lambda i, j, r: (r, j))
    nred = red // br
    o_spec = pl.BlockSpec((bo1, bo2), lambda i, j, r: (i, j))

    def body(a_ref, b_ref, *rest):
        res_ref = rest[0] if res is not None else None
        o_ref, acc_ref = rest[-2:]
        r = pl.program_id(2)

        @pl.when(r == 0)
        def _():
            acc_ref[...] = jnp.zeros_like(acc_ref)

        acc_ref[...] += _dot16(a_ref[...], b_ref[...], mode)

        @pl.when(r == nred - 1)
        def _():
            y = acc_ref[...]
            if scale is not None:
                y = y * scale
            if res_ref is not None:
                y = y + res_ref[...].astype(f32)
            o_ref[...] = y.astype(out_dtype)

    operands = (a, b) if res is None else (a, b, res)
    return pl.pallas_call(
        body, grid=(out[0] // bo1, out[1] // bo2, nred), in_specs=[a_spec, b_spec] + ([o_spec] if res is not None else []),
        out_specs=o_spec, out_shape=SDS(out, out_dtype),
        scratch_shapes=[pltpu.VMEM((bo1, bo2), f32)], name=name, compiler_params=_cparams())(*operands)


def _make_linear(name, out_dtype=f32, scale=None):
    def run(x, w, h):
        return _matmul(x, w.astype(bf16), "nn", name + "_fwd", out_dtype=out_dtype, scale=scale, res=h)

    def grads(x, w, dy):
        dx = _matmul(dy, w.astype(bf16), "nt", name + "_dx", out_dtype=x.dtype, scale=scale)
        return dx, _matmul(x, dy, "tn", name + "_dw", scale=scale)

    @jax.custom_vjp
    def lin(x, w):
        return run(x, w, None)

    lin.defvjp(lambda x, w: (run(x, w, None), (x, w)), lambda res, dy: grads(*res, dy))

    @jax.custom_vjp
    def lin_res(x, w, h):
        return run(x, w, h)

    lin_res.defvjp(lambda x, w, h: (run(x, w, h), (x, w)), lambda res, dy: (*grads(*res, dy), dy))
    return lin, lin_res


def _make_blockop(f, name, out_cols, tm_cands, out_dtypes=None, whole_lead=False):
    out_dtypes = out_dtypes or (f32,) * len(out_cols)

    def specs(arrs, tm, lead, g=None):
        cols = [a if isinstance(a, int) else a.shape[-1] for a in arrs]
        if g is not None:
            return [pl.BlockSpec((g, tm, c), lambda i: (0, i, 0)) for c in cols]
        if lead:
            return [pl.BlockSpec((None, tm, c), lambda g, i: (g, i, 0)) for c in cols]
        return [pl.BlockSpec((tm, c), lambda i: (i, 0)) for c in cols]

    def pspecs(params, lead):
        if lead:
            return [pl.BlockSpec(p.shape, lambda g, i: (0, 0)) for p in params]
        return [pl.BlockSpec(p.shape, lambda i: (0, 0)) for p in params]

    def geometry(ins):
        lead = ins[0].ndim == 3 and not whole_lead
        g = ins[0].shape[0] if ins[0].ndim == 3 and whole_lead else None
        ln = ins[0].shape[-2]
        tm = _pick(ln, tm_cands)
        grid = (ins[0].shape[0], ln // tm) if lead else (ln // tm,)
        return lead, g, tm, grid

    def fwd_call(ins, params):
        lead, g, tm, grid = geometry(ins)
        n_in, n_p = len(ins), len(params)

        def body(*refs):
            rowid = pl.program_id(1 if lead else 0) * tm + lax.broadcasted_iota(jnp.int32, (tm, 1), 0)
            outs = f(rowid, *[r[...].astype(f32) for r in refs[:n_in + n_p]])
            for o_ref, o in zip(refs[n_in + n_p:], outs):
                o_ref[...] = o.astype(o_ref.dtype)

        return pl.pallas_call(
            body, grid=grid, in_specs=specs(ins, tm, lead, g) + pspecs(params, lead),
            out_specs=specs(out_cols, tm, lead, g),
            out_shape=[SDS(ins[0].shape[:-1] + (c,), dt) for c, dt in zip(out_cols, out_dtypes)],
            name=name + "_fwd", compiler_params=_cparams())(*ins, *params)

    def bwd_call(ins, params, cts):
        lead, g, tm, grid = geometry(ins)
        n_in, n_p, n_o = len(ins), len(params), len(cts)

        def body(*refs):
            in_refs = refs[:n_in + n_p]
            ct_refs = refs[n_in + n_p:n_in + n_p + n_o]
            din_refs = refs[n_in + n_p + n_o:n_in + n_p + n_o + n_in]
            dp_refs = refs[n_in + n_p + n_o + n_in:]
            rowid = pl.program_id(1 if lead else 0) * tm + lax.broadcasted_iota(jnp.int32, (tm, 1), 0)
            _, vjp = jax.vjp(lambda *a: tuple(f(rowid, *a)), *[r[...].astype(f32) for r in in_refs])
            grads = vjp(tuple(r[...].astype(f32) for r in ct_refs))
            for r, g in zip(din_refs, grads[:n_in]):
                r[...] = g.astype(r.dtype)
            if n_p:
                first = (pl.program_id(0) == 0) & (pl.program_id(1) == 0) if lead else pl.program_id(0) == 0

                @pl.when(first)
                def _():
                    for r in dp_refs:
                        r[...] = jnp.zeros_like(r)

                for r, g in zip(dp_refs, grads[n_in:]):
                    r[...] += g

        return pl.pallas_call(
            body, grid=grid,
            in_specs=specs(ins, tm, lead, g) + pspecs(params, lead) + specs(cts, tm, lead, g),
            out_specs=specs(ins, tm, lead, g) + pspecs(params, lead),
            out_shape=[SDS(a.shape, a.dtype) for a in ins] + [SDS(p.shape, f32) for p in params],
            name=name + "_bwd", compiler_params=_cparams())(*ins, *params, *cts)

    @jax.custom_vjp
    def op(ins, params):
        return tuple(fwd_call(ins, params))

    def op_fwd(ins, params):
        return tuple(fwd_call(ins, params)), (ins, params)

    def op_bwd(res, cts):
        ins, params = res
        g = bwd_call(ins, params, cts)
        return tuple(g[:len(ins)]), tuple(g[len(ins):])

    op.defvjp(op_fwd, op_bwd)
    return op


def _rowmask(rowid):
    return (rowid >= PAD).astype(f32)


def _rms(x, g):
    return x * lax.rsqrt(jnp.mean(x * x, axis=-1, keepdims=True) + EPS) * g


def _group_mean_sq(x):
    w = x.shape[-1]
    r = lax.broadcasted_iota(jnp.int32, (w, w), 0) // HEAD_DIM
    c = lax.broadcasted_iota(jnp.int32, (w, w), 1) // HEAD_DIM
    return xdot(x * x, jnp.where(r == c, 1.0 / HEAD_DIM, 0.0).astype(f32))


def _f_rmsnorm(rowid, h, g):
    return (_rms(h, g) * _rowmask(rowid),)


def _f_swiglu(rowid, gu):
    half = gu.shape[-1] // 2
    return (jax.nn.silu(gu[:, :half]) * gu[:, half:],)


def _f_headnorm(rowid, o, g):
    return (o * lax.rsqrt(_group_mean_sq(o) + EPS) * g,)


def _f_dn_out(rowid, o, z, g):
    return (o * lax.rsqrt(_group_mean_sq(o) + EPS) * g * jax.nn.silu(z),)


def _f_dn_prep(rowid, conv, ba, alog, dtb):
    tm = conv.shape[0]
    mask = _rowmask(rowid)
    s = jax.nn.silu(conv)
    q, k, v = s[:, :GW], s[:, GW:2 * GW], s[:, 2 * GW:]
    q = q * lax.rsqrt(_group_mean_sq(q) * HEAD_DIM + EPS)
    k = k * lax.rsqrt(_group_mean_sq(k) * HEAD_DIM + EPS)
    beta = jax.nn.sigmoid(ba) * mask
    g = -jnp.exp(alog) * jax.nn.softplus(ba + dtb) * mask
    r = lax.broadcasted_iota(jnp.int32, (tm, tm), 0)
    c = lax.broadcasted_iota(jnp.int32, (tm, tm), 1)
    ltri = jnp.where((r >= c) & (r // 64 == c // 64), 1.0, 0.0).astype(f32)
    gc = xdot(ltri, g)
    er = lax.broadcasted_iota(jnp.int32, (128, GW), 0)
    ec = lax.broadcasted_iota(jnp.int32, (128, GW), 1) // HEAD_DIM
    e_b = jnp.where(er == ec, 1.0, 0.0).astype(f32)
    e_g = jnp.where(er == ec + N_HEADS, 1.0, 0.0).astype(f32)
    return q * mask, k * mask, v * mask, xdot(gc, e_g), xdot(beta, e_b)


def _f_dn_intra(rowid, q, k, v, gc, bb):
    tm = q.shape[0]
    r = lax.broadcasted_iota(jnp.int32, (tm, tm), 0)
    c = lax.broadcasted_iota(jnp.int32, (tm, tm), 1)
    same = r // 64 == c // 64
    incl = same & (r >= c)
    strict = same & (r > c)
    eye = (r == c).astype(f32)
    gcb = jnp.broadcast_to(gc[:, 0:1], (tm, tm))
    gcr = jnp.sum(gcb * eye, axis=0, keepdims=True)
    decay = jnp.where(incl, jnp.exp(jnp.where(incl, gcb - gcr, 0.0)), 0.0)
    qs = q * (HEAD_DIM ** -0.5)
    kb = k * bb
    lmat = jnp.where(strict, bdot_nt(kb, k) * decay, 0.0)
    t = eye - lmat
    p = lmat
    for _ in range(5):
        p = bdot(p, p)
        t = t + bdot(t, p)
    egc = jnp.exp(gc)
    u = bdot(t, v * bb)
    w = bdot(t, kb * egc)
    attn_big = jnp.where(incl, bdot_nt(qs, k) * decay, 0.0)
    attn = attn_big[:, 0:64]
    for b in range(1, tm // 64):
        attn = attn + attn_big[:, 64 * b:64 * b + 64]
    sel = jnp.where(c == (r // 64) * 64 + 63, 1.0, 0.0).astype(f32)
    gl = _select_rows(sel, gc)
    return u, w, qs * egc, k * jnp.exp(gl - gc), attn, jnp.exp(gl)


def _f_dn_intra_heads(rowid, *xs):
    per_head = [_f_dn_intra(rowid, *[x[h] for x in xs]) for h in range(xs[0].shape[0])]
    return tuple(jnp.concatenate([o[j][None] for o in per_head], axis=0) for j in range(len(per_head[0])))


def _f_s5_param(rowid, ar, ai, ldt, bre, bim):
    dt = jnp.exp(ldt)
    mag = jnp.exp(ar * dt)
    abr, abi = mag * jnp.cos(ai * dt), mag * jnp.sin(ai * dt)
    den = ar * ar + ai * ai
    nr = abr - 1.0
    qr = (nr * ar + abi * ai) / den
    qi = (abi * ar - nr * ai) / den
    return abr, abi, qr * bre - qi * bim, qr * bim + qi * bre


def _f_s5_post(rowid, y, u, d, wglu, bglu, gnorm):
    y = jax.nn.gelu(y + d * u)
    o = y * jax.nn.sigmoid(bdot(y, wglu) + bglu)
    return (_rms(o, gnorm),)


def _shift_down(x, halo, r, row8):
    if r == 0:
        return x
    rolled = pltpu.roll(x, r, 0)
    top = jnp.where(row8 < r, pltpu.roll(halo, r, 0), rolled[:8])
    return jnp.concatenate([top, rolled[8:]], axis=0)


def _shift_up(x, halo, r, row8):
    if r == 0:
        return x
    tm = x.shape[0]
    rolled = pltpu.roll(x, tm - r, 0)
    bot = jnp.where(row8 >= 8 - r, pltpu.roll(halo, 8 - r, 0), rolled[tm - 8:])
    return jnp.concatenate([rolled[:tm - 8], bot], axis=0)


def _conv_call(x, w8, mode, name, dc=None):
    ln, ch = x.shape
    tm = _pick(ln, (640, 256, 128))
    n = ln // tm
    t8 = tm // 8

    def body(*refs):
        i = pl.program_id(0)
        row8 = lax.broadcasted_iota(jnp.int32, (8, ch), 0)
        if mode == "fwd":
            x_ref, h_ref, w_ref, o_ref = refs
            halo = jnp.where(i > 0, h_ref[...], 0.0)
            xv = x_ref[...]
            acc = jnp.zeros((tm, ch), f32)
            for j in range(DN_CONV):
                acc = acc + w_ref[j:j + 1, :] * _shift_down(xv, halo, DN_CONV - 1 - j, row8)
            o_ref[...] = acc
        elif mode == "dx":
            x_ref, h_ref, w_ref, o_ref = refs
            halo = jnp.where(i < n - 1, h_ref[...], 0.0)
            xv = x_ref[...]
            acc = jnp.zeros((tm, ch), f32)
            for j in range(DN_CONV):
                acc = acc + w_ref[j:j + 1, :] * _shift_up(xv, halo, DN_CONV - 1 - j, row8)
            o_ref[...] = acc
        else:
            x_ref, h_ref, dc_ref, o_ref = refs
            halo = jnp.where(i > 0, h_ref[...], 0.0)
            xv, dcv = x_ref[...], dc_ref[...]
            acc = jnp.zeros((8, ch), f32)
            for j in range(DN_CONV):
                s = jnp.sum(_shift_down(xv, halo, DN_CONV - 1 - j, row8) * dcv, axis=0, keepdims=True)
                acc = acc + jnp.where(row8 == j, s, 0.0)

            @pl.when(i == 0)
            def _():
                o_ref[...] = jnp.zeros_like(o_ref)

            o_ref[...] += acc

    blk = pl.BlockSpec((tm, ch), lambda i: (i, 0))
    if mode == "dx":
        halo_spec = pl.BlockSpec((8, ch), lambda i: (jnp.minimum((i + 1) * t8, n * t8 - 1), 0))
    else:
        halo_spec = pl.BlockSpec((8, ch), lambda i: (jnp.maximum(i * t8 - 1, 0), 0))
    small = pl.BlockSpec((8, ch), lambda i: (0, 0))
    if mode == "dw":
        return pl.pallas_call(body, grid=(n,), in_specs=[blk, halo_spec, blk], out_specs=small,
                              out_shape=SDS((8, ch), f32), name=name, compiler_params=_cparams())(x, x, dc)
    return pl.pallas_call(body, grid=(n,), in_specs=[blk, halo_spec, small], out_specs=blk,
                          out_shape=SDS((ln, ch), f32), name=name, compiler_params=_cparams())(x, x, w8)


@jax.custom_vjp
def conv_op(x, w8):
    return _conv_call(x, w8, "fwd", "dn_conv_fwd")


def _conv_fwd(x, w8):
    return _conv_call(x, w8, "fwd", "dn_conv_fwd"), (x, w8)


def _conv_bwd(res, dc):
    x, w8 = res
    return _conv_call(dc, w8, "dx", "dn_conv_dx"), _conv_call(x, None, "dw", "dn_conv_dw", dc=dc)


conv_op.defvjp(_conv_fwd, _conv_bwd)


SB_DEAD = -104.0
SB_UNSEEN = -1e30


def _softplus(z):
    return jnp.maximum(z, 0.0) + jnp.log1p(jnp.exp(-jnp.abs(z)))


def _scan_matrix(suffix):
    r = lax.broadcasted_iota(jnp.int32, (256, 256), 0) % 128
    c = lax.broadcasted_iota(jnp.int32, (256, 256), 1)
    inside = (r > c) if suffix else (r < c)
    return jnp.where((c >= 128) | inside, 1.0, 0.0).astype(bf16)


def _block_sums(x, mat):
    hi = x.astype(bf16)
    lo = (x - hi.astype(f32)).astype(bf16)
    r = jnp.dot(jnp.concatenate([hi, lo], axis=1), mat, preferred_element_type=f32)
    return r[:, :128], r[:, 128:]


def _sb_scores(q, k, k0, qpos, suf_mat, c_lk):
    z = lax.dot_general(q, k, _DIMS["nt"], preferred_element_type=f32)
    kpos = k0 + lax.broadcasted_iota(jnp.int32, z.shape, 1)
    valid = (kpos < qpos) & (kpos >= PAD)
    sp = _softplus(z)
    lk = jnp.where(valid, -sp, 0.0)
    suf, tot = _block_sums(lk, suf_mat)
    w = jnp.where(valid, jnp.exp(z - sp + suf + c_lk), 0.0)
    return z, sp, valid, w, tot


def _sb_fwd_call(qs, kb, vb):
    nh, ln, hd = qs.shape
    tq = _pick(ln, (256, 128))
    nsub = tq // 128
    assert nsub == 2
    assert ln // 128 <= 256

    def body(q_ref, k_ref, v_ref, o_ref, c_ref):
        qi = pl.program_id(1)
        q = q_ref[...]
        suf_mat = _scan_matrix(True)
        qpos = qi * tq + lax.broadcasted_iota(jnp.int32, (tq, 128), 0)
        lane = lax.broadcasted_iota(jnp.int32, (tq, 256), 1)

        def live(carry):
            return (carry[0] >= 0) & (carry[1] > 0)

        def step(carry):
            t, _, c_lk, acc, saved = carry
            for tt in (t, t - 1):
                k0 = pl.multiple_of(tt * 128, 128)
                k = k_ref[pl.ds(k0, 128), :]
                v = v_ref[pl.ds(k0, 128), :]
                saved = jnp.where(lane == tt, jnp.concatenate([c_lk, c_lk], axis=1), saved)
                _, _, _, w, tot = _sb_scores(q, k, k0, qpos, suf_mat, c_lk)
                acc = acc + jnp.dot(w.astype(bf16), v, preferred_element_type=f32)
                c_lk = c_lk + tot
            return t - 2, (jnp.max(c_lk) >= SB_DEAD).astype(jnp.int32), c_lk, acc, saved

        init = ((qi + 1) * nsub - 1, jnp.int32(1), jnp.zeros((tq, 128), f32), jnp.zeros((tq, hd), f32),
                jnp.full((tq, 256), SB_UNSEEN, f32))
        _, _, _, acc, saved = lax.while_loop(live, step, init)
        o_ref[...] = acc
        c_ref[...] = saved

    full = pl.BlockSpec((None, ln, hd), lambda h, i: (h, 0, 0))
    blk = pl.BlockSpec((None, tq, hd), lambda h, i: (h, i, 0))
    cblk = pl.BlockSpec((None, tq, 256), lambda h, i: (h, i, 0))
    return pl.pallas_call(body, grid=(nh, ln // tq), in_specs=[blk, full, full], out_specs=[blk, cblk],
                          out_shape=[SDS((nh, ln, hd), f32), SDS((nh, ln, 256), f32)], name="sb_attn_fwd",
                          compiler_params=_cparams())(qs, kb, vb)


def _sb_bwd_call(qs, kb, vb, carries, do):
    nh, ln, hd = qs.shape
    tq = _pick(ln, (256, 128))
    nsub = tq // 128
    assert nsub == 2
    nq = ln // tq

    def body(q_ref, k_ref, v_ref, c_ref, do_ref, dq_ref, dk_hbm, dv_hbm, dk_ref, dv_ref):
        qi = pl.program_id(1)

        @pl.when(qi == 0)
        def _():
            dk_ref[...] = jnp.zeros_like(dk_ref)
            dv_ref[...] = jnp.zeros_like(dv_ref)

        q = q_ref[...]
        dob = do_ref[...].astype(bf16)
        saved = c_ref[...]
        suf_mat = _scan_matrix(True)
        pre_mat = _scan_matrix(False)
        qpos = qi * tq + lax.broadcasted_iota(jnp.int32, (tq, 128), 0)
        lane = lax.broadcasted_iota(jnp.int32, (tq, 256), 1)

        def step(i, carry):
            c_e, dq = carry
            for t in (2 * i, 2 * i + 1):
                k0 = pl.multiple_of(t * 128, 128)
                k = k_ref[pl.ds(k0, 128), :]
                v = v_ref[pl.ds(k0, 128), :]
                c_lk = jnp.sum(jnp.where(lane == t, saved, 0.0), axis=1, keepdims=True)
                z, sp, valid, w, _ = _sb_scores(q, k, k0, qpos, suf_mat, c_lk)
                e = lax.dot_general(dob, v, _DIMS["nt"], preferred_element_type=f32) * w
                pre_e, tot_e = _block_sums(e, pre_mat)
                sig = jnp.exp(z - sp)
                dz = jnp.where(valid, e * (1.0 - sig) - sig * (pre_e + c_e), 0.0).astype(bf16)
                dq = dq + jnp.dot(dz, k, preferred_element_type=f32)
                dk_ref[pl.ds(k0, 128), :] += lax.dot_general(dz, q, _DIMS["tn"], preferred_element_type=f32)
                dv_ref[pl.ds(k0, 128), :] += lax.dot_general(w.astype(bf16), dob, _DIMS["tn"], preferred_element_type=f32)
                c_e = c_e + tot_e
            return c_e, dq

        ntile = (qi + 1) * nsub
        lane1 = lax.broadcasted_iota(jnp.int32, (1, 256), 1)
        dead = (jnp.max(saved, axis=0, keepdims=True) < SB_DEAD) & (lane1 < ntile)
        first = jnp.sum(dead.astype(jnp.int32))
        _, dq = lax.fori_loop(first // 2, ntile // 2, step, (jnp.zeros((tq, 128), f32), jnp.zeros((tq, hd), f32)))
        dq_ref[...] = dq * (HEAD_DIM ** -0.5)

        @pl.when(qi == nq - 1)
        def _():
            pltpu.sync_copy(dk_ref, dk_hbm.at[pl.program_id(0)])
            pltpu.sync_copy(dv_ref, dv_hbm.at[pl.program_id(0)])

    full = pl.BlockSpec((None, ln, hd), lambda h, i: (h, 0, 0))
    blk = pl.BlockSpec((None, tq, hd), lambda h, i: (h, i, 0))
    cblk = pl.BlockSpec((None, tq, 256), lambda h, i: (h, i, 0))
    anyspec = pl.BlockSpec(memory_space=pl.ANY)
    return pl.pallas_call(body, grid=(nh, nq), in_specs=[blk, full, full, cblk, blk],
                          out_specs=[blk, anyspec, anyspec], out_shape=[SDS((nh, ln, hd), f32)] * 3,
                          scratch_shapes=[pltpu.VMEM((ln, hd), f32)] * 2,
                          name="sb_attn_bwd", compiler_params=_cparams())(qs, kb, vb, carries, do)


def _sb_operands(q, k, v):
    return (q * (HEAD_DIM ** -0.5)).astype(bf16), k.astype(bf16), v.astype(bf16)


@jax.custom_vjp
def sb_attention(q, k, v):
    return _sb_fwd_call(*_sb_operands(q, k, v))[0]


def _sb_fwd(q, k, v):
    qs, kb, vb = _sb_operands(q, k, v)
    o, carries = _sb_fwd_call(qs, kb, vb)
    return o, (qs, kb, vb, carries)


def _sb_bwd(res, do):
    return tuple(_sb_bwd_call(*res, do))


sb_attention.defvjp(_sb_fwd, _sb_bwd)


def _dn_scan_geometry(ln):
    tb = _pick(ln, (640, 256))
    return tb, ln // tb, tb // 64


def _dn_scan_fwd_call(u, w, qg, kg, attn, egl):
    nh, ln, hd = u.shape
    tb, nblk, nck = _dn_scan_geometry(ln)

    def body(u_ref, w_ref, qg_ref, kg_ref, a_ref, e_ref, o_ref, hist_ref, s_ref):
        @pl.when(pl.program_id(0) == 0)
        def _():
            s_ref[...] = jnp.zeros_like(s_ref)

        def chunk(ci, _):
            rows = pl.ds(pl.multiple_of(ci * 64, 64), 64)
            for h in range(nh):
                s = s_ref[h]
                hist_ref[ci, h] = s
                v_new = u_ref[h, rows, :] - _dot16(w_ref[h, rows, :], s)
                o_ref[h, rows, :] = _dot16(qg_ref[h, rows, :], s) + _dot16(a_ref[h, rows, :], v_new)
                s_ref[h] = s * e_ref[h, rows, :][0:1, :] + _dot16(kg_ref[h, rows, :], v_new, "tn")
            return 0

        lax.fori_loop(0, nck, chunk, 0)

    blk = pl.BlockSpec((nh, tb, hd), lambda i: (0, i, 0))
    return pl.pallas_call(
        body, grid=(nblk,), in_specs=[blk] * 6,
        out_specs=[blk, pl.BlockSpec((nck, nh, hd, hd), lambda i: (i, 0, 0, 0))],
        out_shape=[SDS((nh, ln, hd), f32), SDS((ln // 64, nh, hd, hd), f32)],
        scratch_shapes=[pltpu.VMEM((nh, hd, hd), f32)], name="dn_scan_fwd", compiler_params=_cparams())(u, w, qg, kg, attn, egl)


def _dn_scan_bwd_call(u, w, qg, kg, attn, egl, hist, do):
    nh, ln, hd = u.shape
    tb, nblk, nck = _dn_scan_geometry(ln)

    def body(u_ref, w_ref, qg_ref, kg_ref, a_ref, e_ref, hist_ref, do_ref,
             du_ref, dw_ref, dqg_ref, dkg_ref, da_ref, de_ref, ds_ref):
        @pl.when(pl.program_id(0) == 0)
        def _():
            ds_ref[...] = jnp.zeros_like(ds_ref)

        row0 = lax.broadcasted_iota(jnp.int32, (64, hd), 0) == 0

        def chunk(i, _):
            ci = nck - 1 - i
            rows = pl.ds(pl.multiple_of(ci * 64, 64), 64)
            for h in range(nh):
                s = hist_ref[ci, h]
                ds = ds_ref[h]
                dov = do_ref[h, rows, :]
                wv, kgv, av = w_ref[h, rows, :], kg_ref[h, rows, :], a_ref[h, rows, :]
                egl_row = e_ref[h, rows, :][0:1, :]
                v_new = u_ref[h, rows, :] - _dot16(wv, s)
                dv_new = _dot16(av, dov, "tn") + _dot16(kgv, ds)
                du_ref[h, rows, :] = dv_new
                dw_ref[h, rows, :] = -_dot16(dv_new, s, "nt")
                dqg_ref[h, rows, :] = _dot16(dov, s, "nt")
                dkg_ref[h, rows, :] = _dot16(v_new, ds, "nt")
                da_ref[h, rows, :] = _dot16(dov, v_new, "nt")
                de_ref[h, rows, :] = jnp.where(row0, jnp.sum(s * ds, axis=0, keepdims=True), 0.0)
                ds_ref[h] = ds * egl_row + _dot16(qg_ref[h, rows, :], dov, "tn") - _dot16(wv, dv_new, "tn")
            return 0

        lax.fori_loop(0, nck, chunk, 0)

    blk = pl.BlockSpec((nh, tb, hd), lambda i: (0, nblk - 1 - i, 0))
    hblk = pl.BlockSpec((nck, nh, hd, hd), lambda i: (nblk - 1 - i, 0, 0, 0))
    return pl.pallas_call(
        body, grid=(nblk,), in_specs=[blk] * 6 + [hblk, blk], out_specs=[blk] * 6,
        out_shape=[SDS((nh, ln, hd), f32)] * 6, scratch_shapes=[pltpu.VMEM((nh, hd, hd), f32)],
        name="dn_scan_bwd", compiler_params=_cparams())(u, w, qg, kg, attn, egl, hist, do)


@jax.custom_vjp
def dn_scan(u, w, qg, kg, attn, egl):
    return _dn_scan_fwd_call(u, w, qg, kg, attn, egl)[0]


def _dn_scan_fwd(u, w, qg, kg, attn, egl):
    o, hist = _dn_scan_fwd_call(u, w, qg, kg, attn, egl)
    return o, (u, w, qg, kg, attn, egl, hist)


def _dn_scan_bwd(res, do):
    return tuple(_dn_scan_bwd_call(*res, do))


dn_scan.defvjp(_dn_scan_fwd, _dn_scan_bwd)


def _cmul(a, b):
    return a[0] * b[0] - a[1] * b[1], a[0] * b[1] + a[1] * b[0]


def _powers(a1):
    a2 = _cmul(a1, a1)
    a3 = _cmul(a2, a1)
    a4 = _cmul(a2, a2)
    return [a1, a2, a3, a4, _cmul(a4, a1), _cmul(a4, a2), _cmul(a4, a3), _cmul(a4, a4)]


def _row_table(pw, row, order):
    tr = jnp.zeros(row.shape, f32)
    ti = jnp.zeros(row.shape, f32)
    for r in range(8):
        p = pw[order(r)]
        tr = tr + jnp.where(row == r, p[0], 0.0)
        ti = ti + jnp.where(row == r, p[1], 0.0)
    return tr, ti


def _s5_fwd_call(u, bmat, cmat, are, aim):
    ln, wu = u.shape
    w2 = bmat.shape[1]
    nl = w2 // 2
    tm = _pick(ln, (256, 128))

    def body(u_ref, b_ref, c_ref, are_ref, aim_ref, st_ref, y_ref, bu_ref, cr_ref, ci_ref):
        @pl.when(pl.program_id(0) == 0)
        def _():
            cr_ref[...] = jnp.zeros_like(cr_ref)
            ci_ref[...] = jnp.zeros_like(ci_ref)

        bu_ref[...] = _dot16(u_ref[...], b_ref[...])
        pw = _powers((are_ref[...], aim_ref[...]))
        row = lax.broadcasted_iota(jnp.int32, (8, nl), 0)
        table = _row_table(pw, row, lambda r: r)

        def tile(t, c):
            rows = pl.ds(pl.multiple_of(t * 8, 8), 8)
            x = (bu_ref[rows, 0:nl], bu_ref[rows, nl:w2])
            for kk in (1, 2, 4):
                sh = (jnp.where(row >= kk, pltpu.roll(x[0], kk, 0), 0.0), jnp.where(row >= kk, pltpu.roll(x[1], kk, 0), 0.0))
                m = _cmul(pw[kk - 1], sh)
                x = (x[0] + m[0], x[1] + m[1])
            m = _cmul(table, c)
            x = (x[0] + m[0], x[1] + m[1])
            st_ref[rows, 0:nl] = x[0]
            st_ref[rows, nl:w2] = x[1]
            return (jnp.sum(jnp.where(row == 7, x[0], 0.0), axis=0, keepdims=True),
                    jnp.sum(jnp.where(row == 7, x[1], 0.0), axis=0, keepdims=True))

        c = lax.fori_loop(0, tm // 8, tile, (cr_ref[...], ci_ref[...]))
        cr_ref[...] = c[0]
        ci_ref[...] = c[1]
        y_ref[...] = _dot16(st_ref[...], c_ref[...])

    vec = pl.BlockSpec((1, nl), lambda i: (0, 0))
    whole = lambda a: pl.BlockSpec(a.shape, lambda i: (0, 0))
    rows = lambda c: pl.BlockSpec((tm, c), lambda i: (i, 0))
    return pl.pallas_call(
        body, grid=(ln // tm,), in_specs=[rows(wu), whole(bmat), whole(cmat), vec, vec], out_specs=[rows(w2), rows(wu)],
        out_shape=[SDS((ln, w2), f32), SDS((ln, wu), f32)],
        scratch_shapes=[pltpu.VMEM((tm, w2), f32), pltpu.VMEM((1, nl), f32), pltpu.VMEM((1, nl), f32)],
        name="s5_fwd", compiler_params=_cparams())(u, bmat, cmat, are, aim)


def _s5_bwd_call(u, bmat, cmat, are, aim, st, dy):
    ln, wu = u.shape
    w2 = bmat.shape[1]
    nl = w2 // 2
    tm = _pick(ln, (256, 128))
    n = ln // tm
    t8 = tm // 8

    def body(u_ref, st_ref, halo_ref, dy_ref, b_hbm, c_hbm, are_ref, aim_ref, du_ref, db_hbm, dc_hbm, dar_ref, dai_ref,
             b_ref, c_ref, db_ref, dc_ref, sb_ref, d_ref, cr_ref, ci_ref):
        i = pl.program_id(0)
        g_ref = d_ref

        @pl.when(i == 0)
        def _():
            pltpu.sync_copy(b_hbm, b_ref)
            pltpu.sync_copy(c_hbm, c_ref)
            db_ref[...] = jnp.zeros_like(db_ref)
            dc_ref[...] = jnp.zeros_like(dc_ref)
            cr_ref[...] = jnp.zeros_like(cr_ref)
            ci_ref[...] = jnp.zeros_like(ci_ref)
            dar_ref[...] = jnp.zeros_like(dar_ref)
            dai_ref[...] = jnp.zeros_like(dai_ref)

        d_ref[...] = _dot16(dy_ref[...], c_ref[...], "nt")
        sb_ref[0:8, :] = jnp.where(i < n - 1, halo_ref[...], 0.0)
        sb_ref[8:tm + 8, :] = st_ref[...]
        pw = _powers((are_ref[...], -aim_ref[...]))
        row = lax.broadcasted_iota(jnp.int32, (8, nl), 0)
        table = _row_table(pw, row, lambda r: 7 - r)

        def tile(j, carry):
            c, acc_r, acc_i = carry
            t = t8 - 1 - j
            rows = pl.ds(pl.multiple_of(t * 8, 8), 8)
            x = (d_ref[rows, 0:nl], d_ref[rows, nl:w2])
            for kk in (1, 2, 4):
                sh = (jnp.where(row < 8 - kk, pltpu.roll(x[0], 8 - kk, 0), 0.0),
                      jnp.where(row < 8 - kk, pltpu.roll(x[1], 8 - kk, 0), 0.0))
                m = _cmul(pw[kk - 1], sh)
                x = (x[0] + m[0], x[1] + m[1])
            m = _cmul(table, c)
            x = (x[0] + m[0], x[1] + m[1])
            g_ref[rows, 0:nl] = x[0]
            g_ref[rows, nl:w2] = x[1]
            prev = (sb_ref[rows, 0:nl], sb_ref[rows, nl:w2])
            cur = (st_ref[rows, 0:nl], st_ref[rows, nl:w2])
            last_r = jnp.sum(jnp.where(row == 7, prev[0], 0.0), axis=0, keepdims=True)
            last_i = jnp.sum(jnp.where(row == 7, prev[1], 0.0), axis=0, keepdims=True)
            sp = (jnp.where(row == 0, last_r, pltpu.roll(cur[0], 1, 0)), jnp.where(row == 0, last_i, pltpu.roll(cur[1], 1, 0)))
            acc_r = acc_r + x[0] * sp[0] + x[1] * sp[1]
            acc_i = acc_i + x[1] * sp[0] - x[0] * sp[1]
            c = (jnp.sum(jnp.where(row == 0, x[0], 0.0), axis=0, keepdims=True),
                 jnp.sum(jnp.where(row == 0, x[1], 0.0), axis=0, keepdims=True))
            return c, acc_r, acc_i

        z8 = jnp.zeros((8, nl), f32)
        c, acc_r, acc_i = lax.fori_loop(0, t8, tile, ((cr_ref[...], ci_ref[...]), z8, z8))
        cr_ref[...] = c[0]
        ci_ref[...] = c[1]
        dar_ref[...] += jnp.sum(acc_r, axis=0, keepdims=True)
        dai_ref[...] += jnp.sum(acc_i, axis=0, keepdims=True)
        g = g_ref[...].astype(bf16)
        du_ref[...] = lax.dot_general(g, b_ref[...], _DIMS["nt"], preferred_element_type=f32)
        db_ref[...] += lax.dot_general(u_ref[...].astype(bf16), g, _DIMS["tn"], preferred_element_type=f32)
        dc_ref[...] += _dot16(st_ref[...], dy_ref[...], "tn")

        @pl.when(i == n - 1)
        def _():
            pltpu.sync_copy(db_ref, db_hbm)
            pltpu.sync_copy(dc_ref, dc_hbm)

    vec = pl.BlockSpec((1, nl), lambda i: (0, 0))
    rows = lambda c: pl.BlockSpec((tm, c), lambda i: (n - 1 - i, 0))
    halo = pl.BlockSpec((8, w2), lambda i: (jnp.maximum((n - 1 - i) * t8 - 1, 0), 0))
    anyspec = pl.BlockSpec(memory_space=pl.ANY)
    return pl.pallas_call(
        body, grid=(n,), in_specs=[rows(wu), rows(w2), halo, rows(wu), anyspec, anyspec, vec, vec],
        out_specs=[rows(wu), anyspec, anyspec, vec, vec],
        out_shape=[SDS((ln, wu), f32), SDS(bmat.shape, f32), SDS(cmat.shape, f32), SDS((1, nl), f32), SDS((1, nl), f32)],
        scratch_shapes=[pltpu.VMEM(bmat.shape, bf16), pltpu.VMEM(cmat.shape, bf16), pltpu.VMEM(bmat.shape, f32),
                        pltpu.VMEM(cmat.shape, f32), pltpu.VMEM((tm + 8, w2), f32), pltpu.VMEM((tm, w2), f32),
                        pltpu.VMEM((1, nl), f32), pltpu.VMEM((1, nl), f32)],
        name="s5_bwd", compiler_params=_cparams())(u, st, st, dy, bmat, cmat, are, aim)


@jax.custom_vjp
def s5_core(u, bmat, cmat, are, aim):
    return _s5_fwd_call(u, bmat.astype(bf16), cmat.astype(bf16), are, aim)[1]


def _s5_core_fwd(u, bmat, cmat, are, aim):
    bb, cb = bmat.astype(bf16), cmat.astype(bf16)
    st, y = _s5_fwd_call(u, bb, cb, are, aim)
    return y, (u, bb, cb, are, aim, st)


def _s5_core_bwd(res, dy):
    return tuple(_s5_bwd_call(*res, dy))


s5_core.defvjp(_s5_core_fwd, _s5_core_bwd)


def _final_call(h, target, gnorm):
    ln, d = h.shape
    tm = PAD + N_META
    skip = (ln - target.shape[0]) // tm

    def body(h_ref, t_ref, g_ref, loss_ref, dh_ref, dg_ref):
        i = pl.program_id(0)

        @pl.when(i == 0)
        def _():
            loss_ref[...] = jnp.zeros_like(loss_ref)
            dg_ref[...] = jnp.zeros_like(dg_ref)

        live = (i >= skip).astype(f32)
        y, vjp = jax.vjp(_rms, h_ref[...], g_ref[...])
        err = (y - t_ref[...]) * live
        loss_ref[...] += 0.5 * jnp.sum(jnp.sum(err * err, axis=1, keepdims=True) / d, axis=0, keepdims=True)
        dh, dg = vjp(err / d)
        dh_ref[...] = dh
        dg_ref[...] += dg

    return pl.pallas_call(
        body, grid=(ln // tm,),
        in_specs=[pl.BlockSpec((tm, d), lambda i: (i, 0)), pl.BlockSpec((tm, d), lambda i: (jnp.maximum(i - skip, 0), 0)),
                  pl.BlockSpec((1, d), lambda i: (0, 0))],
        out_specs=[pl.BlockSpec((1, 1), lambda i: (0, 0)), pl.BlockSpec((tm, d), lambda i: (i, 0)), pl.BlockSpec((1, d), lambda i: (0, 0))],
        out_shape=[SDS((1, 1), f32), SDS((ln, d), f32), SDS((1, d), f32)], name="final_loss", compiler_params=_cparams())(h, target, gnorm)


def _exchange(x, gather, name):
    r, c = x.shape[-2:]

    def body(x_ref, o_ref, send_sems, recv_sems, local_sem):
        ax, ay, ac = lax.axis_index("x"), lax.axis_index("y"), lax.axis_index("c")
        me = 4 * ax + 2 * ay + ac
        local = pltpu.make_async_copy(x_ref if gather else x_ref.at[me], o_ref.at[me], local_sem)
        local.start()
        sent = []
        for k in range(1, N_DEV):
            px = 1 - ax if k & 4 else ax
            py = 1 - ay if k & 2 else ay
            pc = 1 - ac if k & 1 else ac
            p = 4 * px + 2 * py + pc
            cp = pltpu.make_async_remote_copy(
                src_ref=x_ref if gather else x_ref.at[p], dst_ref=o_ref.at[me], send_sem=send_sems.at[k - 1],
                recv_sem=recv_sems.at[k - 1], device_id=(px, py, pc), device_id_type=pl.DeviceIdType.MESH)
            cp.start()
            sent.append((cp, p, (px, py, pc)))
        for k, (cp, p, peer) in enumerate(sent):
            pltpu.make_async_remote_copy(
                src_ref=o_ref.at[p], dst_ref=o_ref.at[p], send_sem=send_sems.at[k], recv_sem=recv_sems.at[k],
                device_id=peer, device_id_type=pl.DeviceIdType.MESH).wait_recv()
        for cp, _, _ in sent:
            cp.wait_send()
        local.wait()

    hbm = pl.BlockSpec(memory_space=pltpu.HBM)
    return pl.pallas_call(
        body, in_specs=[hbm], out_specs=hbm, out_shape=SDS((N_DEV, r, c), x.dtype),
        scratch_shapes=[pltpu.SemaphoreType.DMA((N_DEV - 1,)), pltpu.SemaphoreType.DMA((N_DEV - 1,)), pltpu.SemaphoreType.DMA],
        name=name)(x)


def _gather_via_sibling(x, name):
    r, c = x.shape

    def body(x_ref, o_ref, send_sems, recv_sems, local_sem):
        ax, ay, ac = lax.axis_index("x"), lax.axis_index("y"), lax.axis_index("c")
        me, sibling = (ax, ay, ac), (ax, ay, 1 - ac)
        chips = [(1 - ax, ay), (ax, 1 - ay), (1 - ax, 1 - ay)]

        def slot(px, py, pc):
            return o_ref.at[4 * px + 2 * py + pc]

        def copy(k, block, to, src=None):
            return pltpu.make_async_remote_copy(
                src_ref=slot(*block) if src is None else src, dst_ref=slot(*block), send_sem=send_sems.at[k],
                recv_sem=recv_sems.at[k], device_id=to, device_id_type=pl.DeviceIdType.MESH)

        mine = pltpu.make_async_copy(x_ref, slot(*me), local_sem)
        mine.start()
        first = [copy(0, me, sibling, src=x_ref)] + [copy(1 + j, me, (*chip, ac), src=x_ref) for j, chip in enumerate(chips)]
        for cp in first:
            cp.start()
        passed = [copy(4 + j, (*chip, ac), sibling) for j, chip in enumerate(chips)]
        for j, chip in enumerate(chips):
            copy(1 + j, (*chip, ac), me).wait_recv()
            passed[j].start()
        copy(0, sibling, me).wait_recv()
        for j, chip in enumerate(chips):
            copy(4 + j, (*chip, 1 - ac), me).wait_recv()
        for cp in first + passed:
            cp.wait_send()
        mine.wait()

    hbm = pl.BlockSpec(memory_space=pltpu.HBM)
    return pl.pallas_call(
        body, in_specs=[hbm], out_specs=hbm, out_shape=SDS((N_DEV, r, c), x.dtype),
        scratch_shapes=[pltpu.SemaphoreType.DMA((N_DEV - 1,)), pltpu.SemaphoreType.DMA((N_DEV - 1,)), pltpu.SemaphoreType.DMA],
        name=name)(x)


def _sum_slots(x, name):
    _, r, c = x.shape
    tr = _pick(r, (PACK_BLOCK_ROWS, 256, 128, 64, 32, 16, 8))

    def body(x_ref, o_ref):
        acc = x_ref[0].astype(f32)
        for p in range(1, N_DEV):
            acc = acc + x_ref[p].astype(f32)
        o_ref[...] = acc

    return pl.pallas_call(body, grid=(r // tr,), in_specs=[pl.BlockSpec((N_DEV, tr, c), lambda i: (0, i, 0))],
                          out_specs=pl.BlockSpec((tr, c), lambda i: (i, 0)), out_shape=SDS((r, c), f32), name=name,
                          compiler_params=_cparams())(x)


def _adamw(w, g, m, v, name):
    shape = w.shape
    w, g, m, v = (t.reshape((-1, shape[-1]) if t.ndim > 1 else (1, -1)) for t in (w, g, m, v))
    r, c = w.shape
    tr = _pick(r, (512, 256, 128, 64, 32, 16, 8)) if r % 8 == 0 else r

    def body(w_ref, g_ref, m_ref, v_ref, d_ref, mo_ref, vo_ref):
        gv = g_ref[...]
        mn = ADAM_B1 * m_ref[...] + (1.0 - ADAM_B1) * gv
        vn = ADAM_B2 * v_ref[...] + (1.0 - ADAM_B2) * jnp.square(gv)
        m_hat = mn / (1.0 - ADAM_B1 ** ADAM_STEP)
        v_hat = vn / (1.0 - ADAM_B2 ** ADAM_STEP)
        d_ref[...] = -ADAM_LR * (m_hat / (jnp.sqrt(v_hat) + ADAM_EPS) + ADAM_WD * w_ref[...])
        mo_ref[...] = mn
        vo_ref[...] = vn

    blk = pl.BlockSpec((tr, c), lambda i: (i, 0))
    outs = pl.pallas_call(body, grid=(r // tr,), in_specs=[blk] * 4, out_specs=[blk] * 3, out_shape=[SDS((r, c), f32)] * 3,
                          name=name, compiler_params=_cparams())(w, g, m, v)
    return [t.reshape(shape) for t in outs]


_WEIGHTS = [
    ("meta_tokens", (N_META, D_MODEL), 1), ("ffn1_norm", (DEPTH, D_MODEL), None),
    ("ffn1_w_gate", (DEPTH, D_MODEL, D_FF), 2), ("ffn1_w_up", (DEPTH, D_MODEL, D_FF), 2),
    ("ffn1_w_down", (DEPTH, D_FF, D_MODEL), 1), ("mix_norm", (DEPTH, D_MODEL), None),
    ("w_in", (DEPTH, D_MODEL, IN_WIDTH), 2), ("sb_out_norm", (DEPTH, HEAD_DIM), None),
    ("dn_conv_w", (DEPTH, DN_CONV, 3 * GW), 2), ("dn_a_log", (DEPTH, N_HEADS), None),
    ("dn_dt_bias", (DEPTH, N_HEADS), None), ("dn_out_norm", (DEPTH, HEAD_DIM), None),
    ("s5_a_re", (DEPTH, S5_GROUPS, S5_STATE), None), ("s5_a_im", (DEPTH, S5_GROUPS, S5_STATE), None),
    ("s5_log_dt", (DEPTH, S5_GROUPS), None), ("s5_b_re", (DEPTH, S5_GROUPS, S5_STATE, S5_GROUP), None),
    ("s5_b_im", (DEPTH, S5_GROUPS, S5_STATE, S5_GROUP), None), ("s5_c_re", (DEPTH, S5_GROUPS, S5_GROUP, S5_STATE), None),
    ("s5_c_im", (DEPTH, S5_GROUPS, S5_GROUP, S5_STATE), None), ("s5_d", (DEPTH, S5_WIDTH), None),
    ("s5_w_glu", (DEPTH, S5_WIDTH, S5_WIDTH), 1), ("s5_b_glu", (DEPTH, S5_WIDTH), None),
    ("s5_out_norm", (DEPTH, S5_WIDTH), None), ("w_out", (DEPTH, D_MODEL, D_MODEL), 1),
    ("ffn2_norm", (DEPTH, D_MODEL), None), ("ffn2_w_gate", (DEPTH, D_MODEL, D_FF), 2),
    ("ffn2_w_up", (DEPTH, D_MODEL, D_FF), 2), ("ffn2_w_down", (DEPTH, D_FF, D_MODEL), 1),
    ("final_norm", (D_MODEL,), None),
]
_SHARDED = [(n, s, a) for n, s, a in _WEIGHTS if a is not None]
_REPL = [(n, s) for n, s, a in _WEIGHTS if a is None]
PACK_ROW_ALIGN = 16
PACK_BLOCK_ROWS = 256


def _shard_shape(shape, axis):
    return tuple(d // N_DEV if i == axis else d for i, d in enumerate(shape))


def _pack_rows_of(n):
    rows = -(-n // PACK_COLS)
    return -(-rows // PACK_ROW_ALIGN) * PACK_ROW_ALIGN


def _as_rows(t, lead):
    head = t.shape[:lead]
    n = math.prod(t.shape[lead:])
    rows = _pack_rows_of(n)
    if n % PACK_COLS == 0:
        t = t.reshape(head + (n // PACK_COLS, PACK_COLS))
        return jnp.pad(t, [(0, 0)] * lead + [(0, rows - n // PACK_COLS), (0, 0)])
    t = jnp.pad(t.reshape(head + (n,)), [(0, 0)] * lead + [(0, rows * PACK_COLS - n)])
    return t.reshape(head + (rows, PACK_COLS))


def _pack(parts, lead=0):
    rows = [_as_rows(p, lead) for p in parts]
    total = sum(r.shape[lead] for r in rows)
    fill = -total % PACK_BLOCK_ROWS
    if fill:
        rows.append(jnp.zeros(rows[0].shape[:lead] + (fill, PACK_COLS), rows[0].dtype))
    return jnp.concatenate(rows, axis=lead)


def _unpack(pack, shapes, lead=0):
    out, off = [], 0
    head = pack.shape[:lead]
    for s in shapes:
        n = math.prod(s)
        rows = _pack_rows_of(n)
        blk = lax.slice_in_dim(pack, off, off + rows, axis=lead)
        if n % PACK_COLS == 0:
            out.append(lax.slice_in_dim(blk, 0, n // PACK_COLS, axis=lead).reshape(head + tuple(s)))
        else:
            out.append(blk.reshape(head + (rows * PACK_COLS,))[..., :n].reshape(head + tuple(s)))
        off += rows
    return out


def _unpack_gathered(g):
    blocks = _unpack(g, [_shard_shape(s, a) for _, s, a in _SHARDED], lead=1)
    return {name: jnp.moveaxis(blk.astype(f32), 0, axis).reshape(shape)
            for (name, shape, axis), blk in zip(_SHARDED, blocks)}


def _pack_by_dest(grads):
    parts = []
    for name, shape, axis in _SHARDED:
        ss = _shard_shape(shape, axis)
        g = grads[name].reshape(shape[:axis] + (N_DEV, ss[axis]) + shape[axis + 1:])
        parts.append(jnp.moveaxis(g, axis, 0))
    return _pack(parts, lead=1)


_rmsnorm_op = _make_blockop(_f_rmsnorm, "rmsnorm", (D_MODEL,), (640, 256, 128), (bf16,))
_swiglu_op = _make_blockop(_f_swiglu, "swiglu", (D_FF,), (256, 128), (bf16,))
_headnorm_op = _make_blockop(_f_headnorm, "sb_headnorm", (GW,), (640, 256, 128))
_dn_prep_op = _make_blockop(_f_dn_prep, "dn_prep", (GW,) * 5, (128,))
_dn_intra_op = _make_blockop(_f_dn_intra_heads, "dn_intra", (HEAD_DIM,) * 6, (256,), whole_lead=True)
_dn_out_op = _make_blockop(_f_dn_out, "dn_out", (GW,), (640, 256, 128))
_s5_param_op = _make_blockop(_f_s5_param, "s5_param", (1, 1, S5_GROUP, S5_GROUP), (256,))
_s5_post_op = _make_blockop(_f_s5_post, "s5_post", (S5_WIDTH,), (256, 128))
_lin_gu, _ = _make_linear("ffn_gu", out_dtype=bf16)
_, _lin_down_res = _make_linear("ffn_down", scale=0.5)
_lin_in, _ = _make_linear("mix_in")
_, _lin_out_res = _make_linear("mix_out")


def _heads(t):
    return jnp.transpose(t.reshape(t.shape[0], N_HEADS, HEAD_DIM), (1, 0, 2))


def _unheads(t):
    return jnp.transpose(t, (1, 0, 2)).reshape(t.shape[1], GW)


def _ffn(h, gnorm, wg, wu, wd):
    (xn,) = _rmsnorm_op((h,), (gnorm[None],))
    gu = _lin_gu(xn, jnp.concatenate([wg, wu], axis=1))
    (a,) = _swiglu_op((gu,), ())
    return _lin_down_res(a, wd, h)


def _lane_row(vals, start):
    return jnp.pad(vals, (start, 128 - start - vals.shape[0]))[None]


def _block_diag(t):
    g, a, b = t.shape
    eye = jnp.eye(g, dtype=t.dtype)
    return (t[:, :, None, :] * eye[:, None, :, None]).reshape(g * a, g * b)


def _mixer(h, p):
    (xn,) = _rmsnorm_op((h,), (p["mix_norm"][None],))
    w_in = p["w_in"]
    w_pad = jnp.concatenate([w_in[:, :IN_SMALL], jnp.zeros((D_MODEL, C_S5U - IN_SMALL), f32), w_in[:, IN_SMALL:],
                             jnp.zeros((D_MODEL, IN_PAD - C_S5U - S5_WIDTH), f32)], axis=1)
    proj = _lin_in(xn, w_pad)
    o_sb = sb_attention(_heads(proj[:, C_SBQ:C_SBK]), _heads(proj[:, C_SBK:C_SBV]), _heads(proj[:, C_SBV:C_DNQKV]))
    (o_sb,) = _headnorm_op((_unheads(o_sb),), (jnp.tile(p["sb_out_norm"], N_HEADS)[None],))
    conv = conv_op(proj[:, C_DNQKV:C_DNZ], jnp.pad(p["dn_conv_w"], ((0, 8 - DN_CONV), (0, 0))))
    q, k, v, gc, bb = _dn_prep_op((conv, proj[:, C_DNBA:C_S5U]),
                                  (_lane_row(p["dn_a_log"], N_HEADS), _lane_row(p["dn_dt_bias"], N_HEADS)))
    parts = _dn_intra_op(tuple(_heads(t) for t in (q, k, v, gc, bb)), ())
    o_dn = _unheads(dn_scan(*parts))
    (o_dn,) = _dn_out_op((o_dn, proj[:, C_DNZ:C_DNBA]), (jnp.tile(p["dn_out_norm"], N_HEADS)[None],))
    u = proj[:, C_S5U:C_S5U + S5_WIDTH]
    col = lambda t: t.reshape(S5_LANES, 1)
    abr, abi, bbr, bbi = _s5_param_op(
        (col(p["s5_a_re"]), col(p["s5_a_im"]), col(jnp.repeat(p["s5_log_dt"], S5_STATE)),
         p["s5_b_re"].reshape(S5_LANES, S5_GROUP), p["s5_b_im"].reshape(S5_LANES, S5_GROUP)), ())
    to_b = lambda t: _block_diag(jnp.transpose(t.reshape(S5_GROUPS, S5_STATE, S5_GROUP), (0, 2, 1)))
    to_c = lambda t: _block_diag(jnp.transpose(t, (0, 2, 1)))
    y = s5_core(u, jnp.concatenate([to_b(bbr), to_b(bbi)], axis=1),
                jnp.concatenate([to_c(p["s5_c_re"]), -to_c(p["s5_c_im"])], axis=0),
                abr.reshape(1, S5_LANES), abi.reshape(1, S5_LANES))
    (o_s5,) = _s5_post_op((y, u), (p["s5_d"][None], p["s5_w_glu"], p["s5_b_glu"][None], p["s5_out_norm"][None]))
    return _lin_out_res(jnp.concatenate([o_sb, o_dn, o_s5], axis=1), p["w_out"], h)


def _trunk(x2d, w):
    h = jnp.concatenate([jnp.zeros((PAD, D_MODEL), f32), w["meta_tokens"], x2d], axis=0)
    for l in range(DEPTH):
        p = {k: v[l] for k, v in w.items() if k not in ("meta_tokens", "final_norm")}
        h = _ffn(h, p["ffn1_norm"], p["ffn1_w_gate"], p["ffn1_w_up"], p["ffn1_w_down"])
        h = _mixer(h, p)
        h = _ffn(h, p["ffn2_norm"], p["ffn2_w_gate"], p["ffn2_w_up"], p["ffn2_w_down"])
    return h


def _step(x, loss_target, w, m, v):
    s_names = [n for n, _, _ in _SHARDED]
    r_names = [n for n, _ in _REPL]
    s_shapes = [_shard_shape(s, a) for _, s, a in _SHARDED]
    r_shapes = [s for _, s in _REPL] + [(1,)]
    shard_pack = _pack([w[n] for n in s_names])
    gathered = _gather_via_sibling(shard_pack.astype(bf16), "gather_weights")
    full = _unpack_gathered(gathered)
    full.update({n: w[n] for n, _ in _REPL})
    trunk_w = {k: t for k, t in full.items() if k != "final_norm"}
    h, vjp = jax.vjp(_trunk, x[0], trunk_w)
    loss, dh, dgf = _final_call(h, loss_target[0], full["final_norm"][None])
    dx, dw = vjp(dh)
    dw["final_norm"] = dgf[0]
    g_shard = _sum_slots(_exchange(_pack_by_dest(dw).astype(bf16), False, "scatter_grads"), "sum_shard_grads")
    repl_pack = _pack([dw[n] for n in r_names] + [loss.reshape(1)])
    g_repl = _sum_slots(_exchange(repl_pack, True, "gather_small_grads"), "sum_small_grads")
    outs = {"grad_" + n: t for n, t in zip(s_names, _unpack(g_shard, s_shapes))}
    outs.update({"grad_" + n: t for n, t in zip(r_names + ["loss"], _unpack(g_repl, r_shapes))})
    loss_total = outs["grad_loss"][0]
    names = [n for n, _, _ in _WEIGHTS]
    for n in names:
        outs["delta_" + n], outs["new_m_" + n], outs["new_v_" + n] = _adamw(w[n], outs["grad_" + n], m[n], v[n], "adamw_" + n)
    return (loss_total, dx[None], *[outs["grad_" + n] for n in names], *[outs["delta_" + n] for n in names],
            *[outs["new_m_" + n] for n in names], *[outs["new_v_" + n] for n in names])


def kernel(x, meta_tokens, ffn1_norm, ffn1_w_gate, ffn1_w_up, ffn1_w_down, mix_norm, w_in, sb_out_norm, dn_conv_w, dn_a_log, dn_dt_bias, dn_out_norm, s5_a_re, s5_a_im, s5_log_dt, s5_b_re, s5_b_im, s5_c_re, s5_c_im, s5_d, s5_w_glu, s5_b_glu, s5_out_norm, w_out, ffn2_norm, ffn2_w_gate, ffn2_w_up, ffn2_w_down, final_norm, loss_target, m_meta_tokens, m_ffn1_norm, m_ffn1_w_gate, m_ffn1_w_up, m_ffn1_w_down, m_mix_norm, m_w_in, m_sb_out_norm, m_dn_conv_w, m_dn_a_log, m_dn_dt_bias, m_dn_out_norm, m_s5_a_re, m_s5_a_im, m_s5_log_dt, m_s5_b_re, m_s5_b_im, m_s5_c_re, m_s5_c_im, m_s5_d, m_s5_w_glu, m_s5_b_glu, m_s5_out_norm, m_w_out, m_ffn2_norm, m_ffn2_w_gate, m_ffn2_w_up, m_ffn2_w_down, m_final_norm, v_meta_tokens, v_ffn1_norm, v_ffn1_w_gate, v_ffn1_w_up, v_ffn1_w_down, v_mix_norm, v_w_in, v_sb_out_norm, v_dn_conv_w, v_dn_a_log, v_dn_dt_bias, v_dn_out_norm, v_s5_a_re, v_s5_a_im, v_s5_log_dt, v_s5_b_re, v_s5_b_im, v_s5_c_re, v_s5_c_im, v_s5_d, v_s5_w_glu, v_s5_b_glu, v_s5_out_norm, v_w_out, v_ffn2_norm, v_ffn2_w_gate, v_ffn2_w_up, v_ffn2_w_down, v_final_norm):
    names = [n for n, _, _ in _WEIGHTS]
    ws = (meta_tokens, ffn1_norm, ffn1_w_gate, ffn1_w_up, ffn1_w_down, mix_norm, w_in, sb_out_norm, dn_conv_w, dn_a_log, dn_dt_bias, dn_out_norm, s5_a_re, s5_a_im, s5_log_dt, s5_b_re, s5_b_im, s5_c_re, s5_c_im, s5_d, s5_w_glu, s5_b_glu, s5_out_norm, w_out, ffn2_norm, ffn2_w_gate, ffn2_w_up, ffn2_w_down, final_norm)
    ms = (m_meta_tokens, m_ffn1_norm, m_ffn1_w_gate, m_ffn1_w_up, m_ffn1_w_down, m_mix_norm, m_w_in, m_sb_out_norm, m_dn_conv_w, m_dn_a_log, m_dn_dt_bias, m_dn_out_norm, m_s5_a_re, m_s5_a_im, m_s5_log_dt, m_s5_b_re, m_s5_b_im, m_s5_c_re, m_s5_c_im, m_s5_d, m_s5_w_glu, m_s5_b_glu, m_s5_out_norm, m_w_out, m_ffn2_norm, m_ffn2_w_gate, m_ffn2_w_up, m_ffn2_w_down, m_final_norm)
    vs = (v_meta_tokens, v_ffn1_norm, v_ffn1_w_gate, v_ffn1_w_up, v_ffn1_w_down, v_mix_norm, v_w_in, v_sb_out_norm, v_dn_conv_w, v_dn_a_log, v_dn_dt_bias, v_dn_out_norm, v_s5_a_re, v_s5_a_im, v_s5_log_dt, v_s5_b_re, v_s5_b_im, v_s5_c_re, v_s5_c_im, v_s5_d, v_s5_w_glu, v_s5_b_glu, v_s5_out_norm, v_w_out, v_ffn2_norm, v_ffn2_w_gate, v_ffn2_w_up, v_ffn2_w_down, v_final_norm)
    return _step(x, loss_target, dict(zip(names, ws)), dict(zip(names, ms)), dict(zip(names, vs)))
```

```python
import functools
import math

import jax
import jax.numpy as jnp
from jax import lax
from jax.experimental import pallas as pl
from jax.experimental.pallas import tpu as pltpu

f32 = jnp.float32
bf16 = jnp.bfloat16
HI = lax.Precision.HIGHEST
SDS = jax.ShapeDtypeStruct

N_DEV = 8
D_MODEL = 1024
N_META = 16
PAD = 240
HEAD_DIM = 64
N_HEADS = 4
GW = N_HEADS * HEAD_DIM
DN_CONV = 4
S5_WIDTH = 512
S5_GROUP = 16
S5_GROUPS = 32
S5_STATE = 64
S5_LANES = S5_GROUPS * S5_STATE
D_FF = 2816
DEPTH = 2
EPS = 1e-6
C_SBQ, C_SBK, C_SBV, C_DNQKV, C_DNZ, C_DNBA, C_S5U, IN_PAD = 0, 256, 512, 768, 1536, 1792, 1920, 2560
IN_WIDTH = 2312
IN_SMALL = 1800
VMEM_LIMIT = 56 * 1024 * 1024
PACK_COLS = 512

ADAM_LR, ADAM_B1, ADAM_B2, ADAM_EPS, ADAM_WD, ADAM_STEP = 0.001, 0.9, 0.999, 1e-08, 0.01, 10


def _pick(n, cands):
    for c in cands:
        if n % c == 0:
            return c
    return n


def _cparams():
    return pltpu.CompilerParams(vmem_limit_bytes=VMEM_LIMIT)


_DIMS = {"nn": (((1,), (0,)), ((), ())), "nt": (((1,), (1,)), ((), ())), "tn": (((0,), (0,)), ((), ()))}


def _dot16(a, b, mode="nn"):
    return lax.dot_general(a.astype(bf16), b.astype(bf16), _DIMS[mode], preferred_element_type=f32)


def _make_bdot(mode):
    @jax.custom_vjp
    def f(a, b):
        return _dot16(a, b, mode)

    def fwd(a, b):
        return _dot16(a, b, mode), (a, b)

    def bwd(res, g):
        a, b = res
        if mode == "nn":
            return _dot16(g, b, "nt"), _dot16(a, g, "tn")
        if mode == "nt":
            return _dot16(g, b, "nn"), _dot16(g, a, "tn")
        return _dot16(b, g, "nt"), _dot16(a, g, "nn")

    f.defvjp(fwd, bwd)
    return f


bdot = _make_bdot("nn")
bdot_nt = _make_bdot("nt")
bdot_tn = _make_bdot("tn")


def _split3(x):
    h = x.astype(bf16)
    r = x - h.astype(f32)
    m = r.astype(bf16)
    l = (r - m.astype(f32)).astype(bf16)
    return h, m, l


def _dot3(a, b, mode="nn"):
    ah, am, al = _split3(a)
    bh, bm, bl = _split3(b)
    d = lambda x, y: lax.dot_general(x, y, _DIMS[mode], preferred_element_type=f32)
    return ((d(al, bh) + d(ah, bl)) + d(am, bm)) + ((d(am, bh) + d(ah, bm)) + d(ah, bh))


def _make_xdot(dot):
    @jax.custom_vjp
    def f(a, b):
        return dot(a, b, "nn")

    def fwd(a, b):
        return dot(a, b, "nn"), (a, b)

    def bwd(res, g):
        a, b = res
        return dot(g, b, "nt"), dot(a, g, "tn")

    f.defvjp(fwd, bwd)
    return f


xdot = _make_xdot(_dot3)


def _sel_dot(sel, x, mode):
    s = sel.astype(bf16)
    h, m, l = _split3(x)
    d = lambda y: lax.dot_general(s, y, _DIMS[mode], preferred_element_type=f32)
    return (d(l) + d(m)) + d(h)


@jax.custom_vjp
def _select_rows(sel, x):
    return _sel_dot(sel, x, "nn")


def _select_rows_fwd(sel, x):
    return _sel_dot(sel, x, "nn"), sel


def _select_rows_bwd(sel, g):
    return jnp.zeros_like(sel), _sel_dot(sel, g, "tn")


_select_rows.defvjp(_select_rows_fwd, _select_rows_bwd)


def _matmul(a, b, mode, name, out_dtype=f32, scale=None, res=None):
    row_c = (1280, 640, 512, 384, 256, 128)
    col_c = (1408, 1280, 1024, 512, 256, 128)
    if mode == "nn":
        (m, k), n = a.shape, b.shape[1]
        bo1, bo2, br = _pick(m, row_c), _pick(n, col_c), _pick(k, (1024, 1408, 1280, 512, 256, 128))
        out, red = (m, n), k
        a_spec = pl.BlockSpec((bo1, br), lambda i, j, r: (i, r))
        b_spec = pl.BlockSpec((br, bo2), lambda i, j, r: (r, j))
    elif mode == "nt":
        (m, n), k = a.shape, b.shape[0]
        bo1, bo2, br = _pick(m, row_c), _pick(k, col_c), _pick(n, (1408, 1280, 1024, 512, 256, 128))
        out, red = (m, k), n
        a_spec = pl.BlockSpec((bo1, br), lambda i, j, r: (i, r))
        b_spec = pl.BlockSpec((bo2, br), lambda i, j, r: (j, r))
    else:
        (m, k), n = a.shape, b.shape[1]
        bo1, bo2, br = _pick(k, (1024, 1408, 1280, 512, 256, 128)), _pick(n, col_c), _pick(m, row_c[1:])
        out, red = (k, n), m
        a_spec = pl.BlockSpec((br, bo1), lambda i, j, r: (r, i))
        b_spec = pl.BlockSpec((br, bo2), lambda i, j, r: (r, j))
    nred = red // br
    o_spec = pl.BlockSpec((bo1, bo2), lambda i, j, r: (i, j))

    def body(a_ref, b_ref, *rest):
        res_ref = rest[0] if res is not None else None
        o_ref, acc_ref = rest[-2:]
        r = pl.program_id(2)

        @pl.when(r == 0)
        def _():
            acc_ref[...] = jnp.zeros_like(acc_ref)

        acc_ref[...] += _dot16(a_ref[...], b_ref[...], mode)

        @pl.when(r == nred - 1)
        def _():
            y = acc_ref[...]
            if scale is not None:
                y = y * scale
            if res_ref is not None:
                y = y + res_ref[...].astype(f32)
            o_ref[...] = y.astype(out_dtype)

    operands = (a, b) if res is None else (a, b, res)
    return pl.pallas_call(
        body, grid=(out[0] // bo1, out[1] // bo2, nred), in_specs=[a_spec, b_spec] + ([o_spec] if res is not None else []),
        out_specs=o_spec, out_shape=SDS(out, out_dtype),
        scratch_shapes=[pltpu.VMEM((bo1, bo2), f32)], name=name, compiler_params=_cparams())(*operands)


def _make_linear(name, out_dtype=f32, scale=None):
    def run(x, w, h):
        return _matmul(x, w.astype(bf16), "nn", name + "_fwd", out_dtype=out_dtype, scale=scale, res=h)

    def grads(x, w, dy):
        dx = _matmul(dy, w.astype(bf16), "nt", name + "_dx", out_dtype=x.dtype, scale=scale)
        return dx, _matmul(x, dy, "tn", name + "_dw", scale=scale)

    @jax.custom_vjp
    def lin(x, w):
        return run(x, w, None)

    lin.defvjp(lambda x, w: (run(x, w, None), (x, w)), lambda res, dy: grads(*res, dy))

    @jax.custom_vjp
    def lin_res(x, w, h):
        return run(x, w, h)

    lin_res.defvjp(lambda x, w, h: (run(x, w, h), (x, w)), lambda res, dy: (*grads(*res, dy), dy))
    return lin, lin_res


def _make_blockop(f, name, out_cols, tm_cands, out_dtypes=None, whole_lead=False):
    out_dtypes = out_dtypes or (f32,) * len(out_cols)

    def specs(arrs, tm, lead, g=None):
        cols = [a if isinstance(a, int) else a.shape[-1] for a in arrs]
        if g is not None:
            return [pl.BlockSpec((g, tm, c), lambda i: (0, i, 0)) for c in cols]
        if lead:
            return [pl.BlockSpec((None, tm, c), lambda g, i: (g, i, 0)) for c in cols]
        return [pl.BlockSpec((tm, c), lambda i: (i, 0)) for c in cols]

    def pspecs(params, lead):
        if lead:
            return [pl.BlockSpec(p.shape, lambda g, i: (0, 0)) for p in params]
        return [pl.BlockSpec(p.shape, lambda i: (0, 0)) for p in params]

    def geometry(ins):
        lead = ins[0].ndim == 3 and not whole_lead
        g = ins[0].shape[0] if ins[0].ndim == 3 and whole_lead else None
        ln = ins[0].shape[-2]
        tm = _pick(ln, tm_cands)
        grid = (ins[0].shape[0], ln // tm) if lead else (ln // tm,)
        return lead, g, tm, grid

    def fwd_call(ins, params):
        lead, g, tm, grid = geometry(ins)
        n_in, n_p = len(ins), len(params)

        def body(*refs):
            rowid = pl.program_id(1 if lead else 0) * tm + lax.broadcasted_iota(jnp.int32, (tm, 1), 0)
            outs = f(rowid, *[r[...].astype(f32) for r in refs[:n_in + n_p]])
            for o_ref, o in zip(refs[n_in + n_p:], outs):
                o_ref[...] = o.astype(o_ref.dtype)

        return pl.pallas_call(
            body, grid=grid, in_specs=specs(ins, tm, lead, g) + pspecs(params, lead),
            out_specs=specs(out_cols, tm, lead, g),
            out_shape=[SDS(ins[0].shape[:-1] + (c,), dt) for c, dt in zip(out_cols, out_dtypes)],
            name=name + "_fwd", compiler_params=_cparams())(*ins, *params)

    def bwd_call(ins, params, cts):
        lead, g, tm, grid = geometry(ins)
        n_in, n_p, n_o = len(ins), len(params), len(cts)

        def body(*refs):
            in_refs = refs[:n_in + n_p]
            ct_refs = refs[n_in + n_p:n_in + n_p + n_o]
            din_refs = refs[n_in + n_p + n_o:n_in + n_p + n_o + n_in]
            dp_refs = refs[n_in + n_p + n_o + n_in:]
            rowid = pl.program_id(1 if lead else 0) * tm + lax.broadcasted_iota(jnp.int32, (tm, 1), 0)
            _, vjp = jax.vjp(lambda *a: tuple(f(rowid, *a)), *[r[...].astype(f32) for r in in_refs])
            grads = vjp(tuple(r[...].astype(f32) for r in ct_refs))
            for r, g in zip(din_refs, grads[:n_in]):
                r[...] = g.astype(r.dtype)
            if n_p:
                first = (pl.program_id(0) == 0) & (pl.program_id(1) == 0) if lead else pl.program_id(0) == 0

                @pl.when(first)
                def _():
                    for r in dp_refs:
                        r[...] = jnp.zeros_like(r)

                for r, g in zip(dp_refs, grads[n_in:]):
                    r[...] += g

        return pl.pallas_call(
            body, grid=grid,
            in_specs=specs(ins, tm, lead, g) + pspecs(params, lead) + specs(cts, tm, lead, g),
            out_specs=specs(ins, tm, lead, g) + pspecs(params, lead),
            out_shape=[SDS(a.shape, a.dtype) for a in ins] + [SDS(p.shape, f32) for p in params],
            name=name + "_bwd", compiler_params=_cparams())(*ins, *params, *cts)

    @jax.custom_vjp
    def op(ins, params):
        return tuple(fwd_call(ins, params))

    def op_fwd(ins, params):
        return tuple(fwd_call(ins, params)), (ins, params)

    def op_bwd(res, cts):
        ins, params = res
        g = bwd_call(ins, params, cts)
        return tuple(g[:len(ins)]), tuple(g[len(ins):])

    op.defvjp(op_fwd, op_bwd)
    return op


def _rowmask(rowid):
    return (rowid >= PAD).astype(f32)


def _rms(x, g):
    return x * lax.rsqrt(jnp.mean(x * x, axis=-1, keepdims=True) + EPS) * g


def _group_mean_sq(x):
    w = x.shape[-1]
    r = lax.broadcasted_iota(jnp.int32, (w, w), 0) // HEAD_DIM
    c = lax.broadcasted_iota(jnp.int32, (w, w), 1) // HEAD_DIM
    return xdot(x * x, jnp.where(r == c, 1.0 / HEAD_DIM, 0.0).astype(f32))


def _f_rmsnorm(rowid, h, g):
    return _rms(h, g) * _rowmask(rowid), h


def _f_swiglu(rowid, gu):
    half = gu.shape[-1] // 2
    return (jax.nn.silu(gu[:, :half]) * gu[:, half:],)


def _f_headnorm(rowid, o, g):
    return (o * lax.rsqrt(_group_mean_sq(o) + EPS) * g,)


def _f_dn_out(rowid, o, z, g):
    return (o * lax.rsqrt(_group_mean_sq(o) + EPS) * g * jax.nn.silu(z),)


def _f_dn_prep(rowid, conv, ba, alog, dtb):
    tm = conv.shape[0]
    mask = _rowmask(rowid)
    s = jax.nn.silu(conv)
    q, k, v = s[:, :GW], s[:, GW:2 * GW], s[:, 2 * GW:]
    q = q * lax.rsqrt(_group_mean_sq(q) * HEAD_DIM + EPS)
    k = k * lax.rsqrt(_group_mean_sq(k) * HEAD_DIM + EPS)
    beta = jax.nn.sigmoid(ba) * mask
    g = -jnp.exp(alog) * jax.nn.softplus(ba + dtb) * mask
    r = lax.broadcasted_iota(jnp.int32, (tm, tm), 0)
    c = lax.broadcasted_iota(jnp.int32, (tm, tm), 1)
    ltri = jnp.where((r >= c) & (r // 64 == c // 64), 1.0, 0.0).astype(f32)
    gc = _select_rows(ltri, g)
    er = lax.broadcasted_iota(jnp.int32, (128, GW), 0)
    ec = lax.broadcasted_iota(jnp.int32, (128, GW), 1) // HEAD_DIM
    e_b = jnp.where(er == ec, 1.0, 0.0).astype(f32)
    e_g = jnp.where(er == ec + N_HEADS, 1.0, 0.0).astype(f32)
    return q * mask, k * mask, v * mask, xdot(gc, e_g), xdot(beta, e_b)


def _f_dn_intra(rowid, q, k, v, gc, bb):
    tm = q.shape[0]
    r = lax.broadcasted_iota(jnp.int32, (tm, tm), 0)
    c = lax.broadcasted_iota(jnp.int32, (tm, tm), 1)
    same = r // 64 == c // 64
    incl = same & (r >= c)
    strict = same & (r > c)
    eye = (r == c).astype(f32)
    gcb = jnp.broadcast_to(gc[:, 0:1], (tm, tm))
    gcr = jnp.sum(gcb * eye, axis=0, keepdims=True)
    decay = jnp.where(incl, jnp.exp(jnp.where(incl, gcb - gcr, 0.0)), 0.0)
    qs = q * (HEAD_DIM ** -0.5)
    kb = k * bb
    lmat = jnp.where(strict, bdot_nt(kb, k) * decay, 0.0)
    t = eye - lmat
    p = lmat
    for _ in range(5):
        p = bdot(p, p)
        t = t + bdot(t, p)
    egc = jnp.exp(gc)
    u = bdot(t, v * bb)
    w = bdot(t, kb * egc)
    attn_big = jnp.where(incl, bdot_nt(qs, k) * decay, 0.0)
    attn = attn_big[:, 0:64]
    for b in range(1, tm // 64):
        attn = attn + attn_big[:, 64 * b:64 * b + 64]
    sel = jnp.where(c == (r // 64) * 64 + 63, 1.0, 0.0).astype(f32)
    gl = _select_rows(sel, gc)
    return u, w, qs * egc, k * jnp.exp(gl - gc), attn, jnp.exp(gl)


def _f_dn_intra_heads(rowid, *xs):
    per_head = [_f_dn_intra(rowid, *[x[h] for x in xs]) for h in range(xs[0].shape[0])]
    return tuple(jnp.concatenate([o[j][None] for o in per_head], axis=0) for j in range(len(per_head[0])))


def _f_s5_param(rowid, ar, ai, ldt, bre, bim):
    dt = jnp.exp(ldt)
    mag = jnp.exp(ar * dt)
    abr, abi = mag * jnp.cos(ai * dt), mag * jnp.sin(ai * dt)
    den = ar * ar + ai * ai
    nr = abr - 1.0
    qr = (nr * ar + abi * ai) / den
    qi = (abi * ar - nr * ai) / den
    return abr, abi, qr * bre - qi * bim, qr * bim + qi * bre


def _f_s5_post(rowid, y, u, d, wglu, bglu, gnorm):
    y = jax.nn.gelu(y + d * u)
    o = y * jax.nn.sigmoid(bdot(y, wglu) + bglu)
    return (_rms(o, gnorm),)


def _shift_down(x, halo, r, row8):
    if r == 0:
        return x
    rolled = pltpu.roll(x, r, 0)
    top = jnp.where(row8 < r, pltpu.roll(halo, r, 0), rolled[:8])
    return jnp.concatenate([top, rolled[8:]], axis=0)


def _shift_up(x, halo, r, row8):
    if r == 0:
        return x
    tm = x.shape[0]
    rolled = pltpu.roll(x, tm - r, 0)
    bot = jnp.where(row8 >= 8 - r, pltpu.roll(halo, 8 - r, 0), rolled[tm - 8:])
    return jnp.concatenate([rolled[:tm - 8], bot], axis=0)


def _conv_call(x, w8, mode, name, dc=None):
    ln, ch = x.shape
    tm = _pick(ln, (640, 256, 128))
    n = ln // tm
    t8 = tm // 8

    def body(*refs):
        i = pl.program_id(0)
        row8 = lax.broadcasted_iota(jnp.int32, (8, ch), 0)
        if mode == "fwd":
            x_ref, h_ref, w_ref, o_ref = refs
            halo = jnp.where(i > 0, h_ref[...], 0.0)
            xv = x_ref[...]
            acc = jnp.zeros((tm, ch), f32)
            for j in range(DN_CONV):
                acc = acc + w_ref[j:j + 1, :] * _shift_down(xv, halo, DN_CONV - 1 - j, row8)
            o_ref[...] = acc
        elif mode == "dx":
            x_ref, h_ref, w_ref, o_ref = refs
            halo = jnp.where(i < n - 1, h_ref[...], 0.0)
            xv = x_ref[...]
            acc = jnp.zeros((tm, ch), f32)
            for j in range(DN_CONV):
                acc = acc + w_ref[j:j + 1, :] * _shift_up(xv, halo, DN_CONV - 1 - j, row8)
            o_ref[...] = acc
        else:
            x_ref, h_ref, dc_ref, o_ref = refs
            halo = jnp.where(i > 0, h_ref[...], 0.0)
            xv, dcv = x_ref[...], dc_ref[...]
            acc = jnp.zeros((8, ch), f32)
            for j in range(DN_CONV):
                s = jnp.sum(_shift_down(xv, halo, DN_CONV - 1 - j, row8) * dcv, axis=0, keepdims=True)
                acc = acc + jnp.where(row8 == j, s, 0.0)

            @pl.when(i == 0)
            def _():
                o_ref[...] = jnp.zeros_like(o_ref)

            o_ref[...] += acc

    blk = pl.BlockSpec((tm, ch), lambda i: (i, 0))
    if mode == "dx":
        halo_spec = pl.BlockSpec((8, ch), lambda i: (jnp.minimum((i + 1) * t8, n * t8 - 1), 0))
    else:
        halo_spec = pl.BlockSpec((8, ch), lambda i: (jnp.maximum(i * t8 - 1, 0), 0))
    small = pl.BlockSpec((8, ch), lambda i: (0, 0))
    if mode == "dw":
        return pl.pallas_call(body, grid=(n,), in_specs=[blk, halo_spec, blk], out_specs=small,
                              out_shape=SDS((8, ch), f32), name=name, compiler_params=_cparams())(x, x, dc)
    return pl.pallas_call(body, grid=(n,), in_specs=[blk, halo_spec, small], out_specs=blk,
                          out_shape=SDS((ln, ch), f32), name=name, compiler_params=_cparams())(x, x, w8)


@jax.custom_vjp
def conv_op(x, w8):
    return _conv_call(x, w8, "fwd", "dn_conv_fwd")


def _conv_fwd(x, w8):
    return _conv_call(x, w8, "fwd", "dn_conv_fwd"), (x, w8)


def _conv_bwd(res, dc):
    x, w8 = res
    return _conv_call(dc, w8, "dx", "dn_conv_dx"), _conv_call(x, None, "dw", "dn_conv_dw", dc=dc)


conv_op.defvjp(_conv_fwd, _conv_bwd)


SB_DEAD = -104.0
SB_UNSEEN = -1e30


def _softplus(z):
    return jnp.maximum(z, 0.0) + jnp.log1p(jnp.exp(-jnp.abs(z)))


def _scan_matrix(suffix):
    r = lax.broadcasted_iota(jnp.int32, (256, 256), 0) % 128
    c = lax.broadcasted_iota(jnp.int32, (256, 256), 1)
    inside = (r > c) if suffix else (r < c)
    return jnp.where((c >= 128) | inside, 1.0, 0.0).astype(bf16)


def _block_sums(x, mat):
    hi = x.astype(bf16)
    lo = (x - hi.astype(f32)).astype(bf16)
    r = jnp.dot(jnp.concatenate([hi, lo], axis=1), mat, preferred_element_type=f32)
    return r[:, :128], r[:, 128:]


def _sb_scores(q, k, k0, qpos, suf_mat, c_lk):
    z = lax.dot_general(q, k, _DIMS["nt"], preferred_element_type=f32)
    kpos = k0 + lax.broadcasted_iota(jnp.int32, z.shape, 1)
    valid = (kpos < qpos) & (kpos >= PAD)
    sp = _softplus(z)
    lk = jnp.where(valid, -sp, 0.0)
    suf, tot = _block_sums(lk, suf_mat)
    w = jnp.where(valid, jnp.exp(z - sp + suf + c_lk), 0.0)
    return z, sp, valid, w, tot


def _sb_fwd_call(qs, kb, vb):
    nh, ln, hd = qs.shape
    tq = _pick(ln, (256, 128))
    nsub = tq // 128
    assert nsub == 2
    assert ln // 128 <= 256

    def body(q_ref, k_ref, v_ref, o_ref, c_ref):
        qi = pl.program_id(1)
        q = q_ref[...]
        suf_mat = _scan_matrix(True)
        qpos = qi * tq + lax.broadcasted_iota(jnp.int32, (tq, 128), 0)
        lane = lax.broadcasted_iota(jnp.int32, (tq, 256), 1)

        def live(carry):
            return (carry[0] >= 0) & (carry[1] > 0)

        def step(carry):
            t, _, c_lk, acc, saved = carry
            for tt in (t, t - 1):
                k0 = pl.multiple_of(tt * 128, 128)
                k = k_ref[pl.ds(k0, 128), :]
                v = v_ref[pl.ds(k0, 128), :]
                saved = jnp.where(lane == tt, jnp.concatenate([c_lk, c_lk], axis=1), saved)
                _, _, _, w, tot = _sb_scores(q, k, k0, qpos, suf_mat, c_lk)
                acc = acc + jnp.dot(w.astype(bf16), v, preferred_element_type=f32)
                c_lk = c_lk + tot
            return t - 2, (jnp.max(c_lk) >= SB_DEAD).astype(jnp.int32), c_lk, acc, saved

        init = ((qi + 1) * nsub - 1, jnp.int32(1), jnp.zeros((tq, 128), f32), jnp.zeros((tq, hd), f32),
                jnp.full((tq, 256), SB_UNSEEN, f32))
        _, _, _, acc, saved = lax.while_loop(live, step, init)
        o_ref[...] = acc
        c_ref[...] = saved

    full = pl.BlockSpec((None, ln, hd), lambda h, i: (h, 0, 0))
    blk = pl.BlockSpec((None, tq, hd), lambda h, i: (h, i, 0))
    cblk = pl.BlockSpec((None, tq, 256), lambda h, i: (h, i, 0))
    return pl.pallas_call(body, grid=(nh, ln // tq), in_specs=[blk, full, full], out_specs=[blk, cblk],
                          out_shape=[SDS((nh, ln, hd), f32), SDS((nh, ln, 256), f32)], name="sb_attn_fwd",
                          compiler_params=_cparams())(qs, kb, vb)


def _sb_bwd_call(qs, kb, vb, carries, do):
    nh, ln, hd = qs.shape
    tq = _pick(ln, (256, 128))
    nsub = tq // 128
    assert nsub == 2
    nq = ln // tq

    def body(q_ref, k_ref, v_ref, c_ref, do_ref, dq_ref, dk_hbm, dv_hbm, dk_ref, dv_ref):
        qi = pl.program_id(1)

        @pl.when(qi == 0)
        def _():
            dk_ref[...] = jnp.zeros_like(dk_ref)
            dv_ref[...] = jnp.zeros_like(dv_ref)

        q = q_ref[...]
        dob = do_ref[...].astype(bf16)
        saved = c_ref[...]
        suf_mat = _scan_matrix(True)
        pre_mat = _scan_matrix(False)
        qpos = qi * tq + lax.broadcasted_iota(jnp.int32, (tq, 128), 0)
        lane = lax.broadcasted_iota(jnp.int32, (tq, 256), 1)

        def step(i, carry):
            c_e, dq = carry
            for t in (2 * i, 2 * i + 1):
                k0 = pl.multiple_of(t * 128, 128)
                k = k_ref[pl.ds(k0, 128), :]
                v = v_ref[pl.ds(k0, 128), :]
                c_lk = jnp.sum(jnp.where(lane == t, saved, 0.0), axis=1, keepdims=True)
                z, sp, valid, w, _ = _sb_scores(q, k, k0, qpos, suf_mat, c_lk)
                e = lax.dot_general(dob, v, _DIMS["nt"], preferred_element_type=f32) * w
                pre_e, tot_e = _block_sums(e, pre_mat)
                sig = jnp.exp(z - sp)
                dz = jnp.where(valid, e * (1.0 - sig) - sig * (pre_e + c_e), 0.0).astype(bf16)
                dq = dq + jnp.dot(dz, k, preferred_element_type=f32)
                dk_ref[pl.ds(k0, 128), :] += lax.dot_general(dz, q, _DIMS["tn"], preferred_element_type=f32)
                dv_ref[pl.ds(k0, 128), :] += lax.dot_general(w.astype(bf16), dob, _DIMS["tn"], preferred_element_type=f32)
                c_e = c_e + tot_e
            return c_e, dq

        ntile = (qi + 1) * nsub
        lane1 = lax.broadcasted_iota(jnp.int32, (1, 256), 1)
        dead = (jnp.max(saved, axis=0, keepdims=True) < SB_DEAD) & (lane1 < ntile)
        first = jnp.sum(dead.astype(jnp.int32))
        _, dq = lax.fori_loop(first // 2, ntile // 2, step, (jnp.zeros((tq, 128), f32), jnp.zeros((tq, hd), f32)))
        dq_ref[...] = dq * (HEAD_DIM ** -0.5)

        @pl.when(qi == nq - 1)
        def _():
            pltpu.sync_copy(dk_ref, dk_hbm.at[pl.program_id(0)])
            pltpu.sync_copy(dv_ref, dv_hbm.at[pl.program_id(0)])

    full = pl.BlockSpec((None, ln, hd), lambda h, i: (h, 0, 0))
    blk = pl.BlockSpec((None, tq, hd), lambda h, i: (h, i, 0))
    cblk = pl.BlockSpec((None, tq, 256), lambda h, i: (h, i, 0))
    anyspec = pl.BlockSpec(memory_space=pl.ANY)
    return pl.pallas_call(body, grid=(nh, nq), in_specs=[blk, full, full, cblk, blk],
                          out_specs=[blk, anyspec, anyspec], out_shape=[SDS((nh, ln, hd), f32)] * 3,
                          scratch_shapes=[pltpu.VMEM((ln, hd), f32)] * 2,
                          name="sb_attn_bwd", compiler_params=_cparams())(qs, kb, vb, carries, do)


def _sb_operands(q, k, v):
    return (q * (HEAD_DIM ** -0.5)).astype(bf16), k.astype(bf16), v.astype(bf16)


@jax.custom_vjp
def sb_attention(q, k, v):
    return _sb_fwd_call(*_sb_operands(q, k, v))[0]


def _sb_fwd(q, k, v):
    qs, kb, vb = _sb_operands(q, k, v)
    o, carries = _sb_fwd_call(qs, kb, vb)
    return o, (qs, kb, vb, carries)


def _sb_bwd(res, do):
    return tuple(_sb_bwd_call(*res, do))


sb_attention.defvjp(_sb_fwd, _sb_bwd)


def _dn_scan_geometry(ln):
    tb = _pick(ln, (640, 256))
    return tb, ln // tb, tb // 64


def _dn_scan_fwd_call(u, w, qg, kg, attn, egl):
    nh, ln, hd = u.shape
    tb, nblk, nck = _dn_scan_geometry(ln)

    def body(u_ref, w_ref, qg_ref, kg_ref, a_ref, e_ref, o_ref, hist_ref, s_ref):
        @pl.when(pl.program_id(0) == 0)
        def _():
            s_ref[...] = jnp.zeros_like(s_ref)

        def chunk(ci, _):
            rows = pl.ds(pl.multiple_of(ci * 64, 64), 64)
            for h in range(nh):
                s = s_ref[h]
                hist_ref[ci, h] = s
                v_new = u_ref[h, rows, :] - _dot16(w_ref[h, rows, :], s)
                o_ref[h, rows, :] = _dot16(qg_ref[h, rows, :], s) + _dot16(a_ref[h, rows, :], v_new)
                s_ref[h] = s * e_ref[h, rows, :][0:1, :] + _dot16(kg_ref[h, rows, :], v_new, "tn")
            return 0

        lax.fori_loop(0, nck, chunk, 0)

    blk = pl.BlockSpec((nh, tb, hd), lambda i: (0, i, 0))
    return pl.pallas_call(
        body, grid=(nblk,), in_specs=[blk] * 6,
        out_specs=[blk, pl.BlockSpec((nck, nh, hd, hd), lambda i: (i, 0, 0, 0))],
        out_shape=[SDS((nh, ln, hd), f32), SDS((ln // 64, nh, hd, hd), f32)],
        scratch_shapes=[pltpu.VMEM((nh, hd, hd), f32)], name="dn_scan_fwd", compiler_params=_cparams())(u, w, qg, kg, attn, egl)


def _dn_scan_bwd_call(u, w, qg, kg, attn, egl, hist, do):
    nh, ln, hd = u.shape
    tb, nblk, nck = _dn_scan_geometry(ln)

    def body(u_ref, w_ref, qg_ref, kg_ref, a_ref, e_ref, hist_ref, do_ref,
             du_ref, dw_ref, dqg_ref, dkg_ref, da_ref, de_ref, ds_ref):
        @pl.when(pl.program_id(0) == 0)
        def _():
            ds_ref[...] = jnp.zeros_like(ds_ref)

        row0 = lax.broadcasted_iota(jnp.int32, (64, hd), 0) == 0

        def chunk(i, _):
            ci = nck - 1 - i
            rows = pl.ds(pl.multiple_of(ci * 64, 64), 64)
            for h in range(nh):
                s = hist_ref[ci, h]
                ds = ds_ref[h]
                dov = do_ref[h, rows, :]
                wv, kgv, av = w_ref[h, rows, :], kg_ref[h, rows, :], a_ref[h, rows, :]
                egl_row = e_ref[h, rows, :][0:1, :]
                v_new = u_ref[h, rows, :] - _dot16(wv, s)
                dv_new = _dot16(av, dov, "tn") + _dot16(kgv, ds)
                du_ref[h, rows, :] = dv_new
                dw_ref[h, rows, :] = -_dot16(dv_new, s, "nt")
                dqg_ref[h, rows, :] = _dot16(dov, s, "nt")
                dkg_ref[h, rows, :] = _dot16(v_new, ds, "nt")
                da_ref[h, rows, :] = _dot16(dov, v_new, "nt")
                de_ref[h, rows, :] = jnp.where(row0, jnp.sum(s * ds, axis=0, keepdims=True), 0.0)
                ds_ref[h] = ds * egl_row + _dot16(qg_ref[h, rows, :], dov, "tn") - _dot16(wv, dv_new, "tn")
            return 0

        lax.fori_loop(0, nck, chunk, 0)

    blk = pl.BlockSpec((nh, tb, hd), lambda i: (0, nblk - 1 - i, 0))
    hblk = pl.BlockSpec((nck, nh, hd, hd), lambda i: (nblk - 1 - i, 0, 0, 0))
    return pl.pallas_call(
        body, grid=(nblk,), in_specs=[blk] * 6 + [hblk, blk], out_specs=[blk] * 6,
        out_shape=[SDS((nh, ln, hd), f32)] * 6, scratch_shapes=[pltpu.VMEM((nh, hd, hd), f32)],
        name="dn_scan_bwd", compiler_params=_cparams())(u, w, qg, kg, attn, egl, hist, do)


@jax.custom_vjp
def dn_scan(u, w, qg, kg, attn, egl):
    return _dn_scan_fwd_call(u, w, qg, kg, attn, egl)[0]


def _dn_scan_fwd(u, w, qg, kg, attn, egl):
    o, hist = _dn_scan_fwd_call(u, w, qg, kg, attn, egl)
    return o, (u, w, qg, kg, attn, egl, hist)


def _dn_scan_bwd(res, do):
    return tuple(_dn_scan_bwd_call(*res, do))


dn_scan.defvjp(_dn_scan_fwd, _dn_scan_bwd)


def _cmul(a, b):
    return a[0] * b[0] - a[1] * b[1], a[0] * b[1] + a[1] * b[0]


def _powers(a1):
    a2 = _cmul(a1, a1)
    a3 = _cmul(a2, a1)
    a4 = _cmul(a2, a2)
    return [a1, a2, a3, a4, _cmul(a4, a1), _cmul(a4, a2), _cmul(a4, a3), _cmul(a4, a4)]


def _row_table(pw, row, order):
    tr = jnp.zeros(row.shape, f32)
    ti = jnp.zeros(row.shape, f32)
    for r in range(8):
        p = pw[order(r)]
        tr = tr + jnp.where(row == r, p[0], 0.0)
        ti = ti + jnp.where(row == r, p[1], 0.0)
    return tr, ti


def _s5_fwd_call(u, bmat, cmat, are, aim):
    ln, wu = u.shape
    w2 = bmat.shape[1]
    nl = w2 // 2
    tm = _pick(ln, (256, 128))

    def body(u_ref, b_ref, c_ref, are_ref, aim_ref, st_ref, y_ref, bu_ref, cr_ref, ci_ref):
        @pl.when(pl.program_id(0) == 0)
        def _():
            cr_ref[...] = jnp.zeros_like(cr_ref)
            ci_ref[...] = jnp.zeros_like(ci_ref)

        bu_ref[...] = _dot16(u_ref[...], b_ref[...])
        pw = _powers((are_ref[...], aim_ref[...]))
        row = lax.broadcasted_iota(jnp.int32, (8, nl), 0)
        table = _row_table(pw, row, lambda r: r)

        def tile(t, c):
            rows = pl.ds(pl.multiple_of(t * 8, 8), 8)
            x = (bu_ref[rows, 0:nl], bu_ref[rows, nl:w2])
            for kk in (1, 2, 4):
                sh = (jnp.where(row >= kk, pltpu.roll(x[0], kk, 0), 0.0), jnp.where(row >= kk, pltpu.roll(x[1], kk, 0), 0.0))
                m = _cmul(pw[kk - 1], sh)
                x = (x[0] + m[0], x[1] + m[1])
            m = _cmul(table, c)
            x = (x[0] + m[0], x[1] + m[1])
            st_ref[rows, 0:nl] = x[0]
            st_ref[rows, nl:w2] = x[1]
            return (jnp.sum(jnp.where(row == 7, x[0], 0.0), axis=0, keepdims=True),
                    jnp.sum(jnp.where(row == 7, x[1], 0.0), axis=0, keepdims=True))

        c = lax.fori_loop(0, tm // 8, tile, (cr_ref[...], ci_ref[...]))
        cr_ref[...] = c[0]
        ci_ref[...] = c[1]
        y_ref[...] = _dot16(st_ref[...], c_ref[...])

    vec = pl.BlockSpec((1, nl), lambda i: (0, 0))
    whole = lambda a: pl.BlockSpec(a.shape, lambda i: (0, 0))
    rows = lambda c: pl.BlockSpec((tm, c), lambda i: (i, 0))
    return pl.pallas_call(
        body, grid=(ln // tm,), in_specs=[rows(wu), whole(bmat), whole(cmat), vec, vec], out_specs=[rows(w2), rows(wu)],
        out_shape=[SDS((ln, w2), f32), SDS((ln, wu), f32)],
        scratch_shapes=[pltpu.VMEM((tm, w2), f32), pltpu.VMEM((1, nl), f32), pltpu.VMEM((1, nl), f32)],
        name="s5_fwd", compiler_params=_cparams())(u, bmat, cmat, are, aim)


def _s5_bwd_call(u, bmat, cmat, are, aim, st, dy):
    ln, wu = u.shape
    w2 = bmat.shape[1]
    nl = w2 // 2
    tm = _pick(ln, (256, 128))
    n = ln // tm
    t8 = tm // 8

    def body(u_ref, st_ref, halo_ref, dy_ref, b_hbm, c_hbm, are_ref, aim_ref, du_ref, db_hbm, dc_hbm, dar_ref, dai_ref,
             b_ref, c_ref, db_ref, dc_ref, sb_ref, d_ref, cr_ref, ci_ref):
        i = pl.program_id(0)
        g_ref = d_ref

        @pl.when(i == 0)
        def _():
            pltpu.sync_copy(b_hbm, b_ref)
            pltpu.sync_copy(c_hbm, c_ref)
            db_ref[...] = jnp.zeros_like(db_ref)
            dc_ref[...] = jnp.zeros_like(dc_ref)
            cr_ref[...] = jnp.zeros_like(cr_ref)
            ci_ref[...] = jnp.zeros_like(ci_ref)
            dar_ref[...] = jnp.zeros_like(dar_ref)
            dai_ref[...] = jnp.zeros_like(dai_ref)

        d_ref[...] = _dot16(dy_ref[...], c_ref[...], "nt")
        sb_ref[0:8, :] = jnp.where(i < n - 1, halo_ref[...], 0.0)
        sb_ref[8:tm + 8, :] = st_ref[...]
        pw = _powers((are_ref[...], -aim_ref[...]))
        row = lax.broadcasted_iota(jnp.int32, (8, nl), 0)
        table = _row_table(pw, row, lambda r: 7 - r)

        def tile(j, carry):
            c, acc_r, acc_i = carry
            t = t8 - 1 - j
            rows = pl.ds(pl.multiple_of(t * 8, 8), 8)
            x = (d_ref[rows, 0:nl], d_ref[rows, nl:w2])
            for kk in (1, 2, 4):
                sh = (jnp.where(row < 8 - kk, pltpu.roll(x[0], 8 - kk, 0), 0.0),
                      jnp.where(row < 8 - kk, pltpu.roll(x[1], 8 - kk, 0), 0.0))
                m = _cmul(pw[kk - 1], sh)
                x = (x[0] + m[0], x[1] + m[1])
            m = _cmul(table, c)
            x = (x[0] + m[0], x[1] + m[1])
            g_ref[rows, 0:nl] = x[0]
            g_ref[rows, nl:w2] = x[1]
            prev = (sb_ref[rows, 0:nl], sb_ref[rows, nl:w2])
            cur = (st_ref[rows, 0:nl], st_ref[rows, nl:w2])
            last_r = jnp.sum(jnp.where(row == 7, prev[0], 0.0), axis=0, keepdims=True)
            last_i = jnp.sum(jnp.where(row == 7, prev[1], 0.0), axis=0, keepdims=True)
            sp = (jnp.where(row == 0, last_r, pltpu.roll(cur[0], 1, 0)), jnp.where(row == 0, last_i, pltpu.roll(cur[1], 1, 0)))
            acc_r = acc_r + x[0] * sp[0] + x[1] * sp[1]
            acc_i = acc_i + x[1] * sp[0] - x[0] * sp[1]
            c = (jnp.sum(jnp.where(row == 0, x[0], 0.0), axis=0, keepdims=True),
                 jnp.sum(jnp.where(row == 0, x[1], 0.0), axis=0, keepdims=True))
            return c, acc_r, acc_i

        z8 = jnp.zeros((8, nl), f32)
        c, acc_r, acc_i = lax.fori_loop(0, t8, tile, ((cr_ref[...], ci_ref[...]), z8, z8))
        cr_ref[...] = c[0]
        ci_ref[...] = c[1]
        dar_ref[...] += jnp.sum(acc_r, axis=0, keepdims=True)
        dai_ref[...] += jnp.sum(acc_i, axis=0, keepdims=True)
        g = g_ref[...].astype(bf16)
        du_ref[...] = lax.dot_general(g, b_ref[...], _DIMS["nt"], preferred_element_type=f32)
        db_ref[...] += lax.dot_general(u_ref[...].astype(bf16), g, _DIMS["tn"], preferred_element_type=f32)
        dc_ref[...] += _dot16(st_ref[...], dy_ref[...], "tn")

        @pl.when(i == n - 1)
        def _():
            pltpu.sync_copy(db_ref, db_hbm)
            pltpu.sync_copy(dc_ref, dc_hbm)

    vec = pl.BlockSpec((1, nl), lambda i: (0, 0))
    rows = lambda c: pl.BlockSpec((tm, c), lambda i: (n - 1 - i, 0))
    halo = pl.BlockSpec((8, w2), lambda i: (jnp.maximum((n - 1 - i) * t8 - 1, 0), 0))
    anyspec = pl.BlockSpec(memory_space=pl.ANY)
    return pl.pallas_call(
        body, grid=(n,), in_specs=[rows(wu), rows(w2), halo, rows(wu), anyspec, anyspec, vec, vec],
        out_specs=[rows(wu), anyspec, anyspec, vec, vec],
        out_shape=[SDS((ln, wu), f32), SDS(bmat.shape, f32), SDS(cmat.shape, f32), SDS((1, nl), f32), SDS((1, nl), f32)],
        scratch_shapes=[pltpu.VMEM(bmat.shape, bf16), pltpu.VMEM(cmat.shape, bf16), pltpu.VMEM(bmat.shape, f32),
                        pltpu.VMEM(cmat.shape, f32), pltpu.VMEM((tm + 8, w2), f32), pltpu.VMEM((tm, w2), f32),
                        pltpu.VMEM((1, nl), f32), pltpu.VMEM((1, nl), f32)],
        name="s5_bwd", compiler_params=_cparams())(u, st, st, dy, bmat, cmat, are, aim)


@jax.custom_vjp
def s5_core(u, bmat, cmat, are, aim):
    return _s5_fwd_call(u, bmat.astype(bf16), cmat.astype(bf16), are, aim)[1]


def _s5_core_fwd(u, bmat, cmat, are, aim):
    bb, cb = bmat.astype(bf16), cmat.astype(bf16)
    st, y = _s5_fwd_call(u, bb, cb, are, aim)
    return y, (u, bb, cb, are, aim, st)


def _s5_core_bwd(res, dy):
    return tuple(_s5_bwd_call(*res, dy))


s5_core.defvjp(_s5_core_fwd, _s5_core_bwd)


def _final_call(h, target, gnorm):
    ln, d = h.shape
    tm = PAD + N_META
    skip = (ln - target.shape[0]) // tm

    def body(h_ref, t_ref, g_ref, loss_ref, dh_ref, dg_ref):
        i = pl.program_id(0)

        @pl.when(i == 0)
        def _():
            loss_ref[...] = jnp.zeros_like(loss_ref)
            dg_ref[...] = jnp.zeros_like(dg_ref)

        live = (i >= skip).astype(f32)
        y, vjp = jax.vjp(_rms, h_ref[...], g_ref[...])
        err = (y - t_ref[...]) * live
        loss_ref[...] += 0.5 * jnp.sum(jnp.sum(err * err, axis=1, keepdims=True) / d, axis=0, keepdims=True)
        dh, dg = vjp(err / d)
        dh_ref[...] = dh
        dg_ref[...] += dg

    return pl.pallas_call(
        body, grid=(ln // tm,),
        in_specs=[pl.BlockSpec((tm, d), lambda i: (i, 0)), pl.BlockSpec((tm, d), lambda i: (jnp.maximum(i - skip, 0), 0)),
                  pl.BlockSpec((1, d), lambda i: (0, 0))],
        out_specs=[pl.BlockSpec((1, 1), lambda i: (0, 0)), pl.BlockSpec((tm, d), lambda i: (i, 0)), pl.BlockSpec((1, d), lambda i: (0, 0))],
        out_shape=[SDS((1, 1), f32), SDS((ln, d), f32), SDS((1, d), f32)], name="final_loss", compiler_params=_cparams())(h, target, gnorm)


def _exchange(x, gather, name):
    r, c = x.shape[-2:]

    def body(x_ref, o_ref, send_sems, recv_sems, local_sem):
        ax, ay, ac = lax.axis_index("x"), lax.axis_index("y"), lax.axis_index("c")
        me = 4 * ax + 2 * ay + ac
        local = pltpu.make_async_copy(x_ref if gather else x_ref.at[me], o_ref.at[me], local_sem)
        local.start()
        sent = []
        for k in range(1, N_DEV):
            px = 1 - ax if k & 4 else ax
            py = 1 - ay if k & 2 else ay
            pc = 1 - ac if k & 1 else ac
            p = 4 * px + 2 * py + pc
            cp = pltpu.make_async_remote_copy(
                src_ref=x_ref if gather else x_ref.at[p], dst_ref=o_ref.at[me], send_sem=send_sems.at[k - 1],
                recv_sem=recv_sems.at[k - 1], device_id=(px, py, pc), device_id_type=pl.DeviceIdType.MESH)
            cp.start()
            sent.append((cp, p, (px, py, pc)))
        for k, (cp, p, peer) in enumerate(sent):
            pltpu.make_async_remote_copy(
                src_ref=o_ref.at[p], dst_ref=o_ref.at[p], send_sem=send_sems.at[k], recv_sem=recv_sems.at[k],
                device_id=peer, device_id_type=pl.DeviceIdType.MESH).wait_recv()
        for cp, _, _ in sent:
            cp.wait_send()
        local.wait()

    hbm = pl.BlockSpec(memory_space=pltpu.HBM)
    return pl.pallas_call(
        body, in_specs=[hbm], out_specs=hbm, out_shape=SDS((N_DEV, r, c), x.dtype),
        scratch_shapes=[pltpu.SemaphoreType.DMA((N_DEV - 1,)), pltpu.SemaphoreType.DMA((N_DEV - 1,)), pltpu.SemaphoreType.DMA],
        name=name)(x)


def _gather_via_sibling(x, name):
    r, c = x.shape

    def body(x_ref, o_ref, send_sems, recv_sems, local_sem):
        ax, ay, ac = lax.axis_index("x"), lax.axis_index("y"), lax.axis_index("c")
        me, sibling = (ax, ay, ac), (ax, ay, 1 - ac)
        chips = [(1 - ax, ay), (ax, 1 - ay), (1 - ax, 1 - ay)]

        def slot(px, py, pc):
            return o_ref.at[4 * px + 2 * py + pc]

        def copy(k, block, to, src=None):
            return pltpu.make_async_remote_copy(
                src_ref=slot(*block) if src is None else src, dst_ref=slot(*block), send_sem=send_sems.at[k],
                recv_sem=recv_sems.at[k], device_id=to, device_id_type=pl.DeviceIdType.MESH)

        mine = pltpu.make_async_copy(x_ref, slot(*me), local_sem)
        mine.start()
        first = [copy(0, me, sibling, src=x_ref)] + [copy(1 + j, me, (*chip, ac), src=x_ref) for j, chip in enumerate(chips)]
        for cp in first:
            cp.start()
        passed = [copy(4 + j, (*chip, ac), sibling) for j, chip in enumerate(chips)]
        for j, chip in enumerate(chips):
            copy(1 + j, (*chip, ac), me).wait_recv()
            passed[j].start()
        copy(0, sibling, me).wait_recv()
        for j, chip in enumerate(chips):
            copy(4 + j, (*chip, 1 - ac), me).wait_recv()
        for cp in first + passed:
            cp.wait_send()
        mine.wait()

    hbm = pl.BlockSpec(memory_space=pltpu.HBM)
    return pl.pallas_call(
        body, in_specs=[hbm], out_specs=hbm, out_shape=SDS((N_DEV, r, c), x.dtype),
        scratch_shapes=[pltpu.SemaphoreType.DMA((N_DEV - 1,)), pltpu.SemaphoreType.DMA((N_DEV - 1,)), pltpu.SemaphoreType.DMA],
        name=name)(x)


def _sum_slots(x, name):
    _, r, c = x.shape
    tr = _pick(r, (PACK_BLOCK_ROWS, 256, 128, 64, 32, 16, 8))

    def body(x_ref, o_ref):
        acc = x_ref[0].astype(f32)
        for p in range(1, N_DEV):
            acc = acc + x_ref[p].astype(f32)
        o_ref[...] = acc

    return pl.pallas_call(body, grid=(r // tr,), in_specs=[pl.BlockSpec((N_DEV, tr, c), lambda i: (0, i, 0))],
                          out_specs=pl.BlockSpec((tr, c), lambda i: (i, 0)), out_shape=SDS((r, c), f32), name=name,
                          compiler_params=_cparams())(x)


def _adamw(w, g, m, v, name):
    shape = w.shape
    w, g, m, v = (t.reshape((-1, shape[-1]) if t.ndim > 1 else (1, -1)) for t in (w, g, m, v))
    r, c = w.shape
    tr = _pick(r, (512, 256, 128, 64, 32, 16, 8)) if r % 8 == 0 else r

    def body(w_ref, g_ref, m_ref, v_ref, d_ref, mo_ref, vo_ref):
        gv = g_ref[...]
        mn = ADAM_B1 * m_ref[...] + (1.0 - ADAM_B1) * gv
        vn = ADAM_B2 * v_ref[...] + (1.0 - ADAM_B2) * jnp.square(gv)
        m_hat = mn / (1.0 - ADAM_B1 ** ADAM_STEP)
        v_hat = vn / (1.0 - ADAM_B2 ** ADAM_STEP)
        d_ref[...] = -ADAM_LR * (m_hat / (jnp.sqrt(v_hat) + ADAM_EPS) + ADAM_WD * w_ref[...])
        mo_ref[...] = mn
        vo_ref[...] = vn

    blk = pl.BlockSpec((tr, c), lambda i: (i, 0))
    outs = pl.pallas_call(body, grid=(r // tr,), in_specs=[blk] * 4, out_specs=[blk] * 3, out_shape=[SDS((r, c), f32)] * 3,
                          name=name, compiler_params=_cparams())(w, g, m, v)
    return [t.reshape(shape) for t in outs]


_WEIGHTS = [
    ("meta_tokens", (N_META, D_MODEL), 1), ("ffn1_norm", (DEPTH, D_MODEL), None),
    ("ffn1_w_gate", (DEPTH, D_MODEL, D_FF), 2), ("ffn1_w_up", (DEPTH, D_MODEL, D_FF), 2),
    ("ffn1_w_down", (DEPTH, D_FF, D_MODEL), 1), ("mix_norm", (DEPTH, D_MODEL), None),
    ("w_in", (DEPTH, D_MODEL, IN_WIDTH), 2), ("sb_out_norm", (DEPTH, HEAD_DIM), None),
    ("dn_conv_w", (DEPTH, DN_CONV, 3 * GW), 2), ("dn_a_log", (DEPTH, N_HEADS), None),
    ("dn_dt_bias", (DEPTH, N_HEADS), None), ("dn_out_norm", (DEPTH, HEAD_DIM), None),
    ("s5_a_re", (DEPTH, S5_GROUPS, S5_STATE), None), ("s5_a_im", (DEPTH, S5_GROUPS, S5_STATE), None),
    ("s5_log_dt", (DEPTH, S5_GROUPS), None), ("s5_b_re", (DEPTH, S5_GROUPS, S5_STATE, S5_GROUP), None),
    ("s5_b_im", (DEPTH, S5_GROUPS, S5_STATE, S5_GROUP), None), ("s5_c_re", (DEPTH, S5_GROUPS, S5_GROUP, S5_STATE), None),
    ("s5_c_im", (DEPTH, S5_GROUPS, S5_GROUP, S5_STATE), None), ("s5_d", (DEPTH, S5_WIDTH), None),
    ("s5_w_glu", (DEPTH, S5_WIDTH, S5_WIDTH), 1), ("s5_b_glu", (DEPTH, S5_WIDTH), None),
    ("s5_out_norm", (DEPTH, S5_WIDTH), None), ("w_out", (DEPTH, D_MODEL, D_MODEL), 1),
    ("ffn2_norm", (DEPTH, D_MODEL), None), ("ffn2_w_gate", (DEPTH, D_MODEL, D_FF), 2),
    ("ffn2_w_up", (DEPTH, D_MODEL, D_FF), 2), ("ffn2_w_down", (DEPTH, D_FF, D_MODEL), 1),
    ("final_norm", (D_MODEL,), None),
]
_SHARDED = [(n, s, a) for n, s, a in _WEIGHTS if a is not None]
_REPL = [(n, s) for n, s, a in _WEIGHTS if a is None]
PACK_ROW_ALIGN = 16
PACK_BLOCK_ROWS = 256


def _shard_shape(shape, axis):
    return tuple(d // N_DEV if i == axis else d for i, d in enumerate(shape))


def _pack_rows_of(n):
    rows = -(-n // PACK_COLS)
    return -(-rows // PACK_ROW_ALIGN) * PACK_ROW_ALIGN


def _as_rows(t, lead):
    head = t.shape[:lead]
    n = math.prod(t.shape[lead:])
    rows = _pack_rows_of(n)
    if n % PACK_COLS == 0:
        t = t.reshape(head + (n // PACK_COLS, PACK_COLS))
        return jnp.pad(t, [(0, 0)] * lead + [(0, rows - n // PACK_COLS), (0, 0)])
    t = jnp.pad(t.reshape(head + (n,)), [(0, 0)] * lead + [(0, rows * PACK_COLS - n)])
    return t.reshape(head + (rows, PACK_COLS))


def _pack(parts, lead=0):
    rows = [_as_rows(p, lead) for p in parts]
    total = sum(r.shape[lead] for r in rows)
    fill = -total % PACK_BLOCK_ROWS
    if fill:
        rows.append(jnp.zeros(rows[0].shape[:lead] + (fill, PACK_COLS), rows[0].dtype))
    return jnp.concatenate(rows, axis=lead)


def _unpack(pack, shapes, lead=0):
    out, off = [], 0
    head = pack.shape[:lead]
    for s in shapes:
        n = math.prod(s)
        rows = _pack_rows_of(n)
        blk = lax.slice_in_dim(pack, off, off + rows, axis=lead)
        if n % PACK_COLS == 0:
            out.append(lax.slice_in_dim(blk, 0, n // PACK_COLS, axis=lead).reshape(head + tuple(s)))
        else:
            out.append(blk.reshape(head + (rows * PACK_COLS,))[..., :n].reshape(head + tuple(s)))
        off += rows
    return out


def _unpack_gathered(g):
    blocks = _unpack(g, [_shard_shape(s, a) for _, s, a in _SHARDED], lead=1)
    return {name: jnp.moveaxis(blk.astype(f32), 0, axis).reshape(shape)
            for (name, shape, axis), blk in zip(_SHARDED, blocks)}


def _pack_by_dest(grads):
    parts = []
    for name, shape, axis in _SHARDED:
        ss = _shard_shape(shape, axis)
        g = grads[name].reshape(shape[:axis] + (N_DEV, ss[axis]) + shape[axis + 1:])
        parts.append(jnp.moveaxis(g, axis, 0))
    return _pack(parts, lead=1)


_rmsnorm_op = _make_blockop(_f_rmsnorm, "rmsnorm", (D_MODEL, D_MODEL), (640, 256, 128), (bf16, f32))
_swiglu_op = _make_blockop(_f_swiglu, "swiglu", (D_FF,), (256, 128), (bf16,))
_headnorm_op = _make_blockop(_f_headnorm, "sb_headnorm", (GW,), (640, 256, 128))
_dn_prep_op = _make_blockop(_f_dn_prep, "dn_prep", (GW,) * 5, (128,))
_dn_intra_op = _make_blockop(_f_dn_intra_heads, "dn_intra", (HEAD_DIM,) * 6, (256,), whole_lead=True)
_dn_out_op = _make_blockop(_f_dn_out, "dn_out", (GW,), (640, 256, 128))
_s5_param_op = _make_blockop(_f_s5_param, "s5_param", (1, 1, S5_GROUP, S5_GROUP), (256,))
_s5_post_op = _make_blockop(_f_s5_post, "s5_post", (S5_WIDTH,), (256, 128))
_lin_gu, _ = _make_linear("ffn_gu", out_dtype=bf16)
_, _lin_down_res = _make_linear("ffn_down", scale=0.5)
_lin_in, _ = _make_linear("mix_in")
_, _lin_out_res = _make_linear("mix_out")


def _heads(t):
    return jnp.transpose(t.reshape(t.shape[0], N_HEADS, HEAD_DIM), (1, 0, 2))


def _unheads(t):
    return jnp.transpose(t, (1, 0, 2)).reshape(t.shape[1], GW)


def _ffn(h, gnorm, wg, wu, wd):
    xn, h_skip = _rmsnorm_op((h,), (gnorm[None],))
    gu = _lin_gu(xn, jnp.concatenate([wg, wu], axis=1))
    (a,) = _swiglu_op((gu,), ())
    return _lin_down_res(a, wd, h_skip)


def _lane_row(vals, start):
    return jnp.pad(vals, (start, 128 - start - vals.shape[0]))[None]


def _block_diag(t):
    g, a, b = t.shape
    eye = jnp.eye(g, dtype=t.dtype)
    return (t[:, :, None, :] * eye[:, None, :, None]).reshape(g * a, g * b)


def _mixer(h, p):
    xn, h_skip = _rmsnorm_op((h,), (p["mix_norm"][None],))
    w_in = p["w_in"]
    w_pad = jnp.concatenate([w_in[:, :IN_SMALL], jnp.zeros((D_MODEL, C_S5U - IN_SMALL), f32), w_in[:, IN_SMALL:],
                             jnp.zeros((D_MODEL, IN_PAD - C_S5U - S5_WIDTH), f32)], axis=1)
    proj = _lin_in(xn, w_pad)
    o_sb = sb_attention(_heads(proj[:, C_SBQ:C_SBK]), _heads(proj[:, C_SBK:C_SBV]), _heads(proj[:, C_SBV:C_DNQKV]))
    (o_sb,) = _headnorm_op((_unheads(o_sb),), (jnp.tile(p["sb_out_norm"], N_HEADS)[None],))
    conv = conv_op(proj[:, C_DNQKV:C_DNZ], jnp.pad(p["dn_conv_w"], ((0, 8 - DN_CONV), (0, 0))))
    q, k, v, gc, bb = _dn_prep_op((conv, proj[:, C_DNBA:C_S5U]),
                                  (_lane_row(p["dn_a_log"], N_HEADS), _lane_row(p["dn_dt_bias"], N_HEADS)))
    parts = _dn_intra_op(tuple(_heads(t) for t in (q, k, v, gc, bb)), ())
    o_dn = _unheads(dn_scan(*parts))
    (o_dn,) = _dn_out_op((o_dn, proj[:, C_DNZ:C_DNBA]), (jnp.tile(p["dn_out_norm"], N_HEADS)[None],))
    u = proj[:, C_S5U:C_S5U + S5_WIDTH]
    col = lambda t: t.reshape(S5_LANES, 1)
    abr, abi, bbr, bbi = _s5_param_op(
        (col(p["s5_a_re"]), col(p["s5_a_im"]), col(jnp.repeat(p["s5_log_dt"], S5_STATE)),
         p["s5_b_re"].reshape(S5_LANES, S5_GROUP), p["s5_b_im"].reshape(S5_LANES, S5_GROUP)), ())
    to_b = lambda t: _block_diag(jnp.transpose(t.reshape(S5_GROUPS, S5_STATE, S5_GROUP), (0, 2, 1)))
    to_c = lambda t: _block_diag(jnp.transpose(t, (0, 2, 1)))
    y = s5_core(u, jnp.concatenate([to_b(bbr), to_b(bbi)], axis=1),
                jnp.concatenate([to_c(p["s5_c_re"]), -to_c(p["s5_c_im"])], axis=0),
                abr.reshape(1, S5_LANES), abi.reshape(1, S5_LANES))
    (o_s5,) = _s5_post_op((y, u), (p["s5_d"][None], p["s5_w_glu"], p["s5_b_glu"][None], p["s5_out_norm"][None]))
    return _lin_out_res(jnp.concatenate([o_sb, o_dn, o_s5], axis=1), p["w_out"], h_skip)


def _trunk(x2d, w):
    h = jnp.concatenate([jnp.zeros((PAD, D_MODEL), f32), w["meta_tokens"], x2d], axis=0)
    for l in range(DEPTH):
        p = {k: v[l] for k, v in w.items() if k not in ("meta_tokens", "final_norm")}
        h = _ffn(h, p["ffn1_norm"], p["ffn1_w_gate"], p["ffn1_w_up"], p["ffn1_w_down"])
        h = _mixer(h, p)
        h = _ffn(h, p["ffn2_norm"], p["ffn2_w_gate"], p["ffn2_w_up"], p["ffn2_w_down"])
    return h


def _step(x, loss_target, w, m, v):
    s_names = [n for n, _, _ in _SHARDED]
    r_names = [n for n, _ in _REPL]
    s_shapes = [_shard_shape(s, a) for _, s, a in _SHARDED]
    r_shapes = [s for _, s in _REPL] + [(1,)]
    shard_pack = _pack([w[n] for n in s_names])
    gathered = _gather_via_sibling(shard_pack.astype(bf16), "gather_weights")
    full = _unpack_gathered(gathered)
    full.update({n: w[n] for n, _ in _REPL})
    trunk_w = {k: t for k, t in full.items() if k != "final_norm"}
    h, vjp = jax.vjp(_trunk, x[0], trunk_w)
    loss, dh, dgf = _final_call(h, loss_target[0], full["final_norm"][None])
    dx, dw = vjp(dh)
    dw["final_norm"] = dgf[0]
    g_shard = _sum_slots(_exchange(_pack_by_dest(dw).astype(bf16), False, "scatter_grads"), "sum_shard_grads")
    repl_pack = _pack([dw[n] for n in r_names] + [loss.reshape(1)])
    g_repl = _sum_slots(_exchange(repl_pack, True, "gather_small_grads"), "sum_small_grads")
    outs = {"grad_" + n: t for n, t in zip(s_names, _unpack(g_shard, s_shapes))}
    outs.update({"grad_" + n: t for n, t in zip(r_names + ["loss"], _unpack(g_repl, r_shapes))})
    loss_total = outs["grad_loss"][0]
    names = [n for n, _, _ in _WEIGHTS]
    for n in names:
        outs["delta_" + n], outs["new_m_" + n], outs["new_v_" + n] = _adamw(w[n], outs["grad_" + n], m[n], v[n], "adamw_" + n)
    return (loss_total, dx[None], *[outs["grad_" + n] for n in names], *[outs["delta_" + n] for n in names],
            *[outs["new_m_" + n] for n in names], *[outs["new_v_" + n] for n in names])


def kernel(x, meta_tokens, ffn1_norm, ffn1_w_gate, ffn1_w_up, ffn1_w_down, mix_norm, w_in, sb_out_norm, dn_conv_w, dn_a_log, dn_dt_bias, dn_out_norm, s5_a_re, s5_a_im, s5_log_dt, s5_b_re, s5_b_im, s5_c_re, s5_c_im, s5_d, s5_w_glu, s5_b_glu, s5_out_norm, w_out, ffn2_norm, ffn2_w_gate, ffn2_w_up, ffn2_w_down, final_norm, loss_target, m_meta_tokens, m_ffn1_norm, m_ffn1_w_gate, m_ffn1_w_up, m_ffn1_w_down, m_mix_norm, m_w_in, m_sb_out_norm, m_dn_conv_w, m_dn_a_log, m_dn_dt_bias, m_dn_out_norm, m_s5_a_re, m_s5_a_im, m_s5_log_dt, m_s5_b_re, m_s5_b_im, m_s5_c_re, m_s5_c_im, m_s5_d, m_s5_w_glu, m_s5_b_glu, m_s5_out_norm, m_w_out, m_ffn2_norm, m_ffn2_w_gate, m_ffn2_w_up, m_ffn2_w_down, m_final_norm, v_meta_tokens, v_ffn1_norm, v_ffn1_w_gate, v_ffn1_w_up, v_ffn1_w_down, v_mix_norm, v_w_in, v_sb_out_norm, v_dn_conv_w, v_dn_a_log, v_dn_dt_bias, v_dn_out_norm, v_s5_a_re, v_s5_a_im, v_s5_log_dt, v_s5_b_re, v_s5_b_im, v_s5_c_re, v_s5_c_im, v_s5_d, v_s5_w_glu, v_s5_b_glu, v_s5_out_norm, v_w_out, v_ffn2_norm, v_ffn2_w_gate, v_ffn2_w_up, v_ffn2_w_down, v_final_norm):
    names = [n for n, _, _ in _WEIGHTS]
    ws = (meta_tokens, ffn1_norm, ffn1_w_gate, ffn1_w_up, ffn1_w_down, mix_norm, w_in, sb_out_norm, dn_conv_w, dn_a_log, dn_dt_bias, dn_out_norm, s5_a_re, s5_a_im, s5_log_dt, s5_b_re, s5_b_im, s5_c_re, s5_c_im, s5_d, s5_w_glu, s5_b_glu, s5_out_norm, w_out, ffn2_norm, ffn2_w_gate, ffn2_w_up, ffn2_w_down, final_norm)
    ms = (m_meta_tokens, m_ffn1_norm, m_ffn1_w_gate, m_ffn1_w_up, m_ffn1_w_down, m_mix_norm, m_w_in, m_sb_out_norm, m_dn_conv_w, m_dn_a_log, m_dn_dt_bias, m_dn_out_norm, m_s5_a_re, m_s5_a_im, m_s5_log_dt, m_s5_b_re, m_s5_b_im, m_s5_c_re, m_s5_c_im, m_s5_d, m_s5_w_glu, m_s5_b_glu, m_s5_out_norm, m_w_out, m_ffn2_norm, m_ffn2_w_gate, m_ffn2_w_up, m_ffn2_w_down, m_final_norm)
    vs = (v_meta_tokens, v_ffn1_norm, v_ffn1_w_gate, v_ffn1_w_up, v_ffn1_w_down, v_mix_norm, v_w_in, v_sb_out_norm, v_dn_conv_w, v_dn_a_log, v_dn_dt_bias, v_dn_out_norm, v_s5_a_re, v_s5_a_im, v_s5_log_dt, v_s5_b_re, v_s5_b_im, v_s5_c_re, v_s5_c_im, v_s5_d, v_s5_w_glu, v_s5_b_glu, v_s5_out_norm, v_w_out, v_ffn2_norm, v_ffn2_w_gate, v_ffn2_w_up, v_ffn2_w_down, v_final_norm)
    return _step(x, loss_target, dict(zip(names, ws)), dict(zip(names, ms)), dict(zip(names, vs)))
```

```python
import functools
import math

import jax
import jax.numpy as jnp
from jax import lax
from jax.experimental import pallas as pl
from jax.experimental.pallas import tpu as pltpu

f32 = jnp.float32
bf16 = jnp.bfloat16
HI = lax.Precision.HIGHEST
SDS = jax.ShapeDtypeStruct

N_DEV = 8
D_MODEL = 1024
N_META = 16
PAD = 240
HEAD_DIM = 64
N_HEADS = 4
GW = N_HEADS * HEAD_DIM
DN_CONV = 4
S5_WIDTH = 512
S5_GROUP = 16
S5_GROUPS = 32
S5_STATE = 64
S5_LANES = S5_GROUPS * S5_STATE
D_FF = 2816
DEPTH = 2
EPS = 1e-6
C_SBQ, C_SBK, C_SBV, C_DNQKV, C_DNZ, C_DNBA, C_S5U, IN_PAD = 0, 256, 512, 768, 1536, 1792, 1920, 2560
IN_WIDTH = 2312
IN_SMALL = 1800
VMEM_LIMIT = 56 * 1024 * 1024
PACK_COLS = 512

ADAM_LR, ADAM_B1, ADAM_B2, ADAM_EPS, ADAM_WD, ADAM_STEP = 0.001, 0.9, 0.999, 1e-08, 0.01, 10


def _pick(n, cands):
    for c in cands:
        if n % c == 0:
            return c
    return n


def _cparams():
    return pltpu.CompilerParams(vmem_limit_bytes=VMEM_LIMIT)


_DIMS = {"nn": (((1,), (0,)), ((), ())), "nt": (((1,), (1,)), ((), ())), "tn": (((0,), (0,)), ((), ()))}


def _dot16(a, b, mode="nn"):
    return lax.dot_general(a.astype(bf16), b.astype(bf16), _DIMS[mode], preferred_element_type=f32)


def _make_bdot(mode):
    @jax.custom_vjp
    def f(a, b):
        return _dot16(a, b, mode)

    def fwd(a, b):
        return _dot16(a, b, mode), (a, b)

    def bwd(res, g):
        a, b = res
        if mode == "nn":
            return _dot16(g, b, "nt"), _dot16(a, g, "tn")
        if mode == "nt":
            return _dot16(g, b, "nn"), _dot16(g, a, "tn")
        return _dot16(b, g, "nt"), _dot16(a, g, "nn")

    f.defvjp(fwd, bwd)
    return f


bdot = _make_bdot("nn")
bdot_nt = _make_bdot("nt")
bdot_tn = _make_bdot("tn")


def _split3(x):
    h = x.astype(bf16)
    r = x - h.astype(f32)
    m = r.astype(bf16)
    l = (r - m.astype(f32)).astype(bf16)
    return h, m, l


def _dot3(a, b, mode="nn"):
    ah, am, al = _split3(a)
    bh, bm, bl = _split3(b)
    d = lambda x, y: lax.dot_general(x, y, _DIMS[mode], preferred_element_type=f32)
    return ((d(al, bh) + d(ah, bl)) + d(am, bm)) + ((d(am, bh) + d(ah, bm)) + d(ah, bh))


def _make_xdot(dot):
    @jax.custom_vjp
    def f(a, b):
        return dot(a, b, "nn")

    def fwd(a, b):
        return dot(a, b, "nn"), (a, b)

    def bwd(res, g):
        a, b = res
        return dot(g, b, "nt"), dot(a, g, "tn")

    f.defvjp(fwd, bwd)
    return f


xdot = _make_xdot(_dot3)


def _sel_dot(sel, x, mode):
    s = sel.astype(bf16)
    h, m, l = _split3(x)
    d = lambda y: lax.dot_general(s, y, _DIMS[mode], preferred_element_type=f32)
    return (d(l) + d(m)) + d(h)


@jax.custom_vjp
def _select_rows(sel, x):
    return _sel_dot(sel, x, "nn")


def _select_rows_fwd(sel, x):
    return _sel_dot(sel, x, "nn"), sel


def _select_rows_bwd(sel, g):
    return jnp.zeros_like(sel), _sel_dot(sel, g, "tn")


_select_rows.defvjp(_select_rows_fwd, _select_rows_bwd)


def _matmul(a, b, mode, name, out_dtype=f32, scale=None, res=None):
    row_c = (1280, 640, 512, 384, 256, 128)
    col_c = (1408, 1280, 1024, 512, 256, 128)
    if mode == "nn":
        (m, k), n = a.shape, b.shape[1]
        bo1, bo2, br = _pick(m, row_c), _pick(n, col_c), _pick(k, (1024, 1408, 1280, 512, 256, 128))
        out, red = (m, n), k
        a_spec = pl.BlockSpec((bo1, br), lambda i, j, r: (i, r))
        b_spec = pl.BlockSpec((br, bo2), lambda i, j, r: (r, j))
    elif mode == "nt":
        (m, n), k = a.shape, b.shape[0]
        bo1, bo2, br = _pick(m, row_c), _pick(k, col_c), _pick(n, (1408, 1280, 1024, 512, 256, 128))
        out, red = (m, k), n
        a_spec = pl.BlockSpec((bo1, br), lambda i, j, r: (i, r))
        b_spec = pl.BlockSpec((bo2, br), lambda i, j, r: (j, r))
    else:
        (m, k), n = a.shape, b.shape[1]
        bo1, bo2, br = _pick(k, (1024, 1408, 1280, 512, 256, 128)), _pick(n, col_c), _pick(m, row_c[1:])
        out, red = (k, n), m
        a_spec = pl.BlockSpec((br, bo1), lambda i, j, r: (r, i))
        b_spec = pl.BlockSpec((br, bo2), lambda i, j, r: (r, j))
    nred = red // br
    o_spec = pl.BlockSpec((bo1, bo2), lambda i, j, r: (i, j))

    def body(a_ref, b_ref, *rest):
        res_ref = rest[0] if res is not None else None
        o_ref, acc_ref = rest[-2:]
        r = pl.program_id(2)

        @pl.when(r == 0)
        def _():
            acc_ref[...] = jnp.zeros_like(acc_ref)

        acc_ref[...] += _dot16(a_ref[...], b_ref[...], mode)

        @pl.when(r == nred - 1)
        def _():
            y = acc_ref[...]
            if scale is not None:
                y = y * scale
            if res_ref is not None:
                y = y + res_ref[...].astype(f32)
            o_ref[...] = y.astype(out_dtype)

    operands = (a, b) if res is None else (a, b, res)
    return pl.pallas_call(
        body, grid=(out[0] // bo1, out[1] // bo2, nred), in_specs=[a_spec, b_spec] + ([o_spec] if res is not None else []),
        out_specs=o_spec, out_shape=SDS(out, out_dtype),
        scratch_shapes=[pltpu.VMEM((bo1, bo2), f32)], name=name, compiler_params=_cparams())(*operands)


def _make_linear(name, out_dtype=f32, scale=None):
    def run(x, w, h):
        return _matmul(x, w.astype(bf16), "nn", name + "_fwd", out_dtype=out_dtype, scale=scale, res=h)

    def grads(x, w, dy):
        dx = _matmul(dy, w.astype(bf16), "nt", name + "_dx", out_dtype=x.dtype, scale=scale)
        return dx, _matmul(x, dy, "tn", name + "_dw", scale=scale)

    @jax.custom_vjp
    def lin(x, w):
        return run(x, w, None)

    lin.defvjp(lambda x, w: (run(x, w, None), (x, w)), lambda res, dy: grads(*res, dy))

    @jax.custom_vjp
    def lin_res(x, w, h):
        return run(x, w, h)

    lin_res.defvjp(lambda x, w, h: (run(x, w, h), (x, w)), lambda res, dy: (*grads(*res, dy), dy))
    return lin, lin_res


def _make_blockop(f, name, out_cols, tm_cands, out_dtypes=None, whole_lead=False):
    out_dtypes = out_dtypes or (f32,) * len(out_cols)

    def specs(arrs, tm, lead, g=None):
        cols = [a if isinstance(a, int) else a.shape[-1] for a in arrs]
        if g is not None:
            return [pl.BlockSpec((g, tm, c), lambda i: (0, i, 0)) for c in cols]
        if lead:
            return [pl.BlockSpec((None, tm, c), lambda g, i: (g, i, 0)) for c in cols]
        return [pl.BlockSpec((tm, c), lambda i: (i, 0)) for c in cols]

    def pspecs(params, lead):
        if lead:
            return [pl.BlockSpec(p.shape, lambda g, i: (0, 0)) for p in params]
        return [pl.BlockSpec(p.shape, lambda i: (0, 0)) for p in params]

    def geometry(ins):
        lead = ins[0].ndim == 3 and not whole_lead
        g = ins[0].shape[0] if ins[0].ndim == 3 and whole_lead else None
        ln = ins[0].shape[-2]
        tm = _pick(ln, tm_cands)
        grid = (ins[0].shape[0], ln // tm) if lead else (ln // tm,)
        return lead, g, tm, grid

    def fwd_call(ins, params):
        lead, g, tm, grid = geometry(ins)
        n_in, n_p = len(ins), len(params)

        def body(*refs):
            rowid = pl.program_id(1 if lead else 0) * tm + lax.broadcasted_iota(jnp.int32, (tm, 1), 0)
            outs = f(rowid, *[r[...].astype(f32) for r in refs[:n_in + n_p]])
            for o_ref, o in zip(refs[n_in + n_p:], outs):
                o_ref[...] = o.astype(o_ref.dtype)

        return pl.pallas_call(
            body, grid=grid, in_specs=specs(ins, tm, lead, g) + pspecs(params, lead),
            out_specs=specs(out_cols, tm, lead, g),
            out_shape=[SDS(ins[0].shape[:-1] + (c,), dt) for c, dt in zip(out_cols, out_dtypes)],
            name=name + "_fwd", compiler_params=_cparams())(*ins, *params)

    def bwd_call(ins, params, cts):
        lead, g, tm, grid = geometry(ins)
        n_in, n_p, n_o = len(ins), len(params), len(cts)

        def body(*refs):
            in_refs = refs[:n_in + n_p]
            ct_refs = refs[n_in + n_p:n_in + n_p + n_o]
            din_refs = refs[n_in + n_p + n_o:n_in + n_p + n_o + n_in]
            dp_refs = refs[n_in + n_p + n_o + n_in:]
            rowid = pl.program_id(1 if lead else 0) * tm + lax.broadcasted_iota(jnp.int32, (tm, 1), 0)
            _, vjp = jax.vjp(lambda *a: tuple(f(rowid, *a)), *[r[...].astype(f32) for r in in_refs])
            grads = vjp(tuple(r[...].astype(f32) for r in ct_refs))
            for r, g in zip(din_refs, grads[:n_in]):
                r[...] = g.astype(r.dtype)
            if n_p:
                first = (pl.program_id(0) == 0) & (pl.program_id(1) == 0) if lead else pl.program_id(0) == 0

                @pl.when(first)
                def _():
                    for r in dp_refs:
                        r[...] = jnp.zeros_like(r)

                for r, g in zip(dp_refs, grads[n_in:]):
                    r[...] += g

        return pl.pallas_call(
            body, grid=grid,
            in_specs=specs(ins, tm, lead, g) + pspecs(params, lead) + specs(cts, tm, lead, g),
            out_specs=specs(ins, tm, lead, g) + pspecs(params, lead),
            out_shape=[SDS(a.shape, a.dtype) for a in ins] + [SDS(p.shape, f32) for p in params],
            name=name + "_bwd", compiler_params=_cparams())(*ins, *params, *cts)

    @jax.custom_vjp
    def op(ins, params):
        return tuple(fwd_call(ins, params))

    def op_fwd(ins, params):
        return tuple(fwd_call(ins, params)), (ins, params)

    def op_bwd(res, cts):
        ins, params = res
        g = bwd_call(ins, params, cts)
        return tuple(g[:len(ins)]), tuple(g[len(ins):])

    op.defvjp(op_fwd, op_bwd)
    return op


def _rowmask(rowid):
    return (rowid >= PAD).astype(f32)


def _rms(x, g):
    return x * lax.rsqrt(jnp.mean(x * x, axis=-1, keepdims=True) + EPS) * g


def _group_mean_sq(x):
    w = x.shape[-1]
    r = lax.broadcasted_iota(jnp.int32, (w, w), 0) // HEAD_DIM
    c = lax.broadcasted_iota(jnp.int32, (w, w), 1) // HEAD_DIM
    return xdot(x * x, jnp.where(r == c, 1.0 / HEAD_DIM, 0.0).astype(f32))


def _f_rmsnorm(rowid, h, g):
    return _rms(h, g) * _rowmask(rowid), h


def _f_swiglu(rowid, gu):
    half = gu.shape[-1] // 2
    return (jax.nn.silu(gu[:, :half]) * gu[:, half:],)


def _f_headnorm(rowid, o, g):
    return (o * lax.rsqrt(_group_mean_sq(o) + EPS) * g,)


def _f_dn_out(rowid, o, z, g):
    return (o * lax.rsqrt(_group_mean_sq(o) + EPS) * g * jax.nn.silu(z),)


def _f_dn_prep(rowid, conv, ba, alog, dtb):
    tm = conv.shape[0]
    mask = _rowmask(rowid)
    s = jax.nn.silu(conv)
    q, k, v = s[:, :GW], s[:, GW:2 * GW], s[:, 2 * GW:]
    q = q * lax.rsqrt(_group_mean_sq(q) * HEAD_DIM + EPS)
    k = k * lax.rsqrt(_group_mean_sq(k) * HEAD_DIM + EPS)
    beta = jax.nn.sigmoid(ba) * mask
    g = -jnp.exp(alog) * jax.nn.softplus(ba + dtb) * mask
    r = lax.broadcasted_iota(jnp.int32, (tm, tm), 0)
    c = lax.broadcasted_iota(jnp.int32, (tm, tm), 1)
    ltri = jnp.where((r >= c) & (r // 64 == c // 64), 1.0, 0.0).astype(f32)
    gc = _select_rows(ltri, g)
    er = lax.broadcasted_iota(jnp.int32, (128, GW), 0)
    ec = lax.broadcasted_iota(jnp.int32, (128, GW), 1) // HEAD_DIM
    e_b = jnp.where(er == ec, 1.0, 0.0).astype(f32)
    e_g = jnp.where(er == ec + N_HEADS, 1.0, 0.0).astype(f32)
    return q * mask, k * mask, v * mask, xdot(gc, e_g), xdot(beta, e_b)


def _f_dn_intra(rowid, q, k, v, gc, bb):
    tm = q.shape[0]
    r = lax.broadcasted_iota(jnp.int32, (tm, tm), 0)
    c = lax.broadcasted_iota(jnp.int32, (tm, tm), 1)
    same = r // 64 == c // 64
    incl = same & (r >= c)
    strict = same & (r > c)
    eye = (r == c).astype(f32)
    gcb = jnp.broadcast_to(gc[:, 0:1], (tm, tm))
    gcr = jnp.sum(gcb * eye, axis=0, keepdims=True)
    decay = jnp.where(incl, jnp.exp(jnp.where(incl, gcb - gcr, 0.0)), 0.0)
    qs = q * (HEAD_DIM ** -0.5)
    kb = k * bb
    lmat = jnp.where(strict, bdot_nt(kb, k) * decay, 0.0)
    t = eye - lmat
    p = lmat
    for _ in range(5):
        p = bdot(p, p)
        t = t + bdot(t, p)
    egc = jnp.exp(gc)
    u = bdot(t, v * bb)
    w = bdot(t, kb * egc)
    attn_big = jnp.where(incl, bdot_nt(qs, k) * decay, 0.0)
    attn = attn_big[:, 0:64]
    for b in range(1, tm // 64):
        attn = attn + attn_big[:, 64 * b:64 * b + 64]
    sel = jnp.where(c == (r // 64) * 64 + 63, 1.0, 0.0).astype(f32)
    gl = _select_rows(sel, gc)
    return u, w, qs * egc, k * jnp.exp(gl - gc), attn, jnp.exp(gl)


def _f_dn_intra_heads(rowid, *xs):
    per_head = [_f_dn_intra(rowid, *[x[h] for x in xs]) for h in range(xs[0].shape[0])]
    return tuple(jnp.concatenate([o[j][None] for o in per_head], axis=0) for j in range(len(per_head[0])))


def _f_s5_param(rowid, ar, ai, ldt, bre, bim):
    dt = jnp.exp(ldt)
    mag = jnp.exp(ar * dt)
    abr, abi = mag * jnp.cos(ai * dt), mag * jnp.sin(ai * dt)
    den = ar * ar + ai * ai
    nr = abr - 1.0
    qr = (nr * ar + abi * ai) / den
    qi = (abi * ar - nr * ai) / den
    return abr, abi, qr * bre - qi * bim, qr * bim + qi * bre


def _f_s5_post(rowid, y, u, d, wglu, bglu, gnorm):
    y = jax.nn.gelu(y + d * u)
    o = y * jax.nn.sigmoid(bdot(y, wglu) + bglu)
    return (_rms(o, gnorm),)


def _shift_down(x, halo, r, row8):
    if r == 0:
        return x
    rolled = pltpu.roll(x, r, 0)
    top = jnp.where(row8 < r, pltpu.roll(halo, r, 0), rolled[:8])
    return jnp.concatenate([top, rolled[8:]], axis=0)


def _shift_up(x, halo, r, row8):
    if r == 0:
        return x
    tm = x.shape[0]
    rolled = pltpu.roll(x, tm - r, 0)
    bot = jnp.where(row8 >= 8 - r, pltpu.roll(halo, 8 - r, 0), rolled[tm - 8:])
    return jnp.concatenate([rolled[:tm - 8], bot], axis=0)


def _conv_call(x, w8, mode, name, dc=None):
    ln, ch = x.shape
    tm = _pick(ln, (640, 256, 128))
    n = ln // tm
    t8 = tm // 8

    def body(*refs):
        i = pl.program_id(0)
        row8 = lax.broadcasted_iota(jnp.int32, (8, ch), 0)
        if mode == "fwd":
            x_ref, h_ref, w_ref, o_ref = refs
            halo = jnp.where(i > 0, h_ref[...], 0.0)
            xv = x_ref[...]
            acc = jnp.zeros((tm, ch), f32)
            for j in range(DN_CONV):
                acc = acc + w_ref[j:j + 1, :] * _shift_down(xv, halo, DN_CONV - 1 - j, row8)
            o_ref[...] = acc
        elif mode == "dx":
            x_ref, h_ref, w_ref, o_ref = refs
            halo = jnp.where(i < n - 1, h_ref[...], 0.0)
            xv = x_ref[...]
            acc = jnp.zeros((tm, ch), f32)
            for j in range(DN_CONV):
                acc = acc + w_ref[j:j + 1, :] * _shift_up(xv, halo, DN_CONV - 1 - j, row8)
            o_ref[...] = acc
        else:
            x_ref, h_ref, dc_ref, o_ref = refs
            halo = jnp.where(i > 0, h_ref[...], 0.0)
            xv, dcv = x_ref[...], dc_ref[...]
            acc = jnp.zeros((8, ch), f32)
            for j in range(DN_CONV):
                s = jnp.sum(_shift_down(xv, halo, DN_CONV - 1 - j, row8) * dcv, axis=0, keepdims=True)
                acc = acc + jnp.where(row8 == j, s, 0.0)

            @pl.when(i == 0)
            def _():
                o_ref[...] = jnp.zeros_like(o_ref)

            o_ref[...] += acc

    blk = pl.BlockSpec((tm, ch), lambda i: (i, 0))
    if mode == "dx":
        halo_spec = pl.BlockSpec((8, ch), lambda i: (jnp.minimum((i + 1) * t8, n * t8 - 1), 0))
    else:
        halo_spec = pl.BlockSpec((8, ch), lambda i: (jnp.maximum(i * t8 - 1, 0), 0))
    small = pl.BlockSpec((8, ch), lambda i: (0, 0))
    if mode == "dw":
        return pl.pallas_call(body, grid=(n,), in_specs=[blk, halo_spec, blk], out_specs=small,
                              out_shape=SDS((8, ch), f32), name=name, compiler_params=_cparams())(x, x, dc)
    return pl.pallas_call(body, grid=(n,), in_specs=[blk, halo_spec, small], out_specs=blk,
                          out_shape=SDS((ln, ch), f32), name=name, compiler_params=_cparams())(x, x, w8)


@jax.custom_vjp
def conv_op(x, w8):
    return _conv_call(x, w8, "fwd", "dn_conv_fwd")


def _conv_fwd(x, w8):
    return _conv_call(x, w8, "fwd", "dn_conv_fwd"), (x, w8)


def _conv_bwd(res, dc):
    x, w8 = res
    return _conv_call(dc, w8, "dx", "dn_conv_dx"), _conv_call(x, None, "dw", "dn_conv_dw", dc=dc)


conv_op.defvjp(_conv_fwd, _conv_bwd)


SB_DEAD = -104.0
SB_UNSEEN = -1e30


def _softplus(z):
    return jnp.maximum(z, 0.0) + jnp.log1p(jnp.exp(-jnp.abs(z)))


def _scan_matrix(suffix):
    r = lax.broadcasted_iota(jnp.int32, (256, 256), 0) % 128
    c = lax.broadcasted_iota(jnp.int32, (256, 256), 1)
    inside = (r > c) if suffix else (r < c)
    return jnp.where((c >= 128) | inside, 1.0, 0.0).astype(bf16)


def _block_sums(x, mat):
    hi = x.astype(bf16)
    lo = (x - hi.astype(f32)).astype(bf16)
    r = jnp.dot(jnp.concatenate([hi, lo], axis=1), mat, preferred_element_type=f32)
    return r[:, :128], r[:, 128:]


def _sb_scores(q, k, k0, qpos, suf_mat, c_lk):
    z = lax.dot_general(q, k, _DIMS["nt"], preferred_element_type=f32)
    kpos = k0 + lax.broadcasted_iota(jnp.int32, z.shape, 1)
    valid = (kpos < qpos) & (kpos >= PAD)
    sp = _softplus(z)
    lk = jnp.where(valid, -sp, 0.0)
    suf, tot = _block_sums(lk, suf_mat)
    w = jnp.where(valid, jnp.exp(z - sp + suf + c_lk), 0.0)
    return z, sp, valid, w, tot


def _sb_fwd_call(qs, kb, vb):
    nh, ln, hd = qs.shape
    tq = _pick(ln, (256, 128))
    nsub = tq // 128
    assert nsub == 2
    assert ln // 128 <= 256

    def body(q_ref, k_ref, v_ref, o_ref, c_ref):
        qi = pl.program_id(1)
        q = q_ref[...]
        suf_mat = _scan_matrix(True)
        qpos = qi * tq + lax.broadcasted_iota(jnp.int32, (tq, 128), 0)
        lane = lax.broadcasted_iota(jnp.int32, (tq, 256), 1)

        def live(carry):
            return (carry[0] >= 0) & (carry[1] > 0)

        def step(carry):
            t, _, c_lk, acc, saved = carry
            for tt in (t, t - 1):
                k0 = pl.multiple_of(tt * 128, 128)
                k = k_ref[pl.ds(k0, 128), :]
                v = v_ref[pl.ds(k0, 128), :]
                saved = jnp.where(lane == tt, jnp.concatenate([c_lk, c_lk], axis=1), saved)
                _, _, _, w, tot = _sb_scores(q, k, k0, qpos, suf_mat, c_lk)
                acc = acc + jnp.dot(w.astype(bf16), v, preferred_element_type=f32)
                c_lk = c_lk + tot
            return t - 2, (jnp.max(c_lk) >= SB_DEAD).astype(jnp.int32), c_lk, acc, saved

        init = ((qi + 1) * nsub - 1, jnp.int32(1), jnp.zeros((tq, 128), f32), jnp.zeros((tq, hd), f32),
                jnp.full((tq, 256), SB_UNSEEN, f32))
        _, _, _, acc, saved = lax.while_loop(live, step, init)
        o_ref[...] = acc
        c_ref[...] = saved

    full = pl.BlockSpec((None, ln, hd), lambda h, i: (h, 0, 0))
    blk = pl.BlockSpec((None, tq, hd), lambda h, i: (h, i, 0))
    cblk = pl.BlockSpec((None, tq, 256), lambda h, i: (h, i, 0))
    return pl.pallas_call(body, grid=(nh, ln // tq), in_specs=[blk, full, full], out_specs=[blk, cblk],
                          out_shape=[SDS((nh, ln, hd), f32), SDS((nh, ln, 256), f32)], name="sb_attn_fwd",
                          compiler_params=_cparams())(qs, kb, vb)


def _sb_bwd_call(qs, kb, vb, carries, do):
    nh, ln, hd = qs.shape
    tq = _pick(ln, (256, 128))
    nsub = tq // 128
    assert nsub == 2
    nq = ln // tq

    def body(q_ref, k_ref, v_ref, c_ref, do_ref, dq_ref, dk_hbm, dv_hbm, dk_ref, dv_ref):
        qi = pl.program_id(1)

        @pl.when(qi == 0)
        def _():
            dk_ref[...] = jnp.zeros_like(dk_ref)
            dv_ref[...] = jnp.zeros_like(dv_ref)

        q = q_ref[...]
        dob = do_ref[...].astype(bf16)
        saved = c_ref[...]
        suf_mat = _scan_matrix(True)
        pre_mat = _scan_matrix(False)
        qpos = qi * tq + lax.broadcasted_iota(jnp.int32, (tq, 128), 0)
        lane = lax.broadcasted_iota(jnp.int32, (tq, 256), 1)

        def step(i, carry):
            c_e, dq = carry
            for t in (2 * i, 2 * i + 1):
                k0 = pl.multiple_of(t * 128, 128)
                k = k_ref[pl.ds(k0, 128), :]
                v = v_ref[pl.ds(k0, 128), :]
                c_lk = jnp.sum(jnp.where(lane == t, saved, 0.0), axis=1, keepdims=True)
                z, sp, valid, w, _ = _sb_scores(q, k, k0, qpos, suf_mat, c_lk)
                e = lax.dot_general(dob, v, _DIMS["nt"], preferred_element_type=f32) * w
                pre_e, tot_e = _block_sums(e, pre_mat)
                sig = jnp.exp(z - sp)
                dz = jnp.where(valid, e * (1.0 - sig) - sig * (pre_e + c_e), 0.0).astype(bf16)
                dq = dq + jnp.dot(dz, k, preferred_element_type=f32)
                dk_ref[pl.ds(k0, 128), :] += lax.dot_general(dz, q, _DIMS["tn"], preferred_element_type=f32)
                dv_ref[pl.ds(k0, 128), :] += lax.dot_general(w.astype(bf16), dob, _DIMS["tn"], preferred_element_type=f32)
                c_e = c_e + tot_e
            return c_e, dq

        ntile = (qi + 1) * nsub
        lane1 = lax.broadcasted_iota(jnp.int32, (1, 256), 1)
        dead = (jnp.max(saved, axis=0, keepdims=True) < SB_DEAD) & (lane1 < ntile)
        first = jnp.sum(dead.astype(jnp.int32))
        _, dq = lax.fori_loop(first // 2, ntile // 2, step, (jnp.zeros((tq, 128), f32), jnp.zeros((tq, hd), f32)))
        dq_ref[...] = dq * (HEAD_DIM ** -0.5)

        @pl.when(qi == nq - 1)
        def _():
            pltpu.sync_copy(dk_ref, dk_hbm.at[pl.program_id(0)])
            pltpu.sync_copy(dv_ref, dv_hbm.at[pl.program_id(0)])

    full = pl.BlockSpec((None, ln, hd), lambda h, i: (h, 0, 0))
    blk = pl.BlockSpec((None, tq, hd), lambda h, i: (h, i, 0))
    cblk = pl.BlockSpec((None, tq, 256), lambda h, i: (h, i, 0))
    anyspec = pl.BlockSpec(memory_space=pl.ANY)
    return pl.pallas_call(body, grid=(nh, nq), in_specs=[blk, full, full, cblk, blk],
                          out_specs=[blk, anyspec, anyspec], out_shape=[SDS((nh, ln, hd), f32)] * 3,
                          scratch_shapes=[pltpu.VMEM((ln, hd), f32)] * 2,
                          name="sb_attn_bwd", compiler_params=_cparams())(qs, kb, vb, carries, do)


def _sb_operands(q, k, v):
    return (q * (HEAD_DIM ** -0.5)).astype(bf16), k.astype(bf16), v.astype(bf16)


@jax.custom_vjp
def sb_attention(q, k, v):
    return _sb_fwd_call(*_sb_operands(q, k, v))[0]


def _sb_fwd(q, k, v):
    qs, kb, vb = _sb_operands(q, k, v)
    o, carries = _sb_fwd_call(qs, kb, vb)
    return o, (qs, kb, vb, carries)


def _sb_bwd(res, do):
    return tuple(_sb_bwd_call(*res, do))


sb_attention.defvjp(_sb_fwd, _sb_bwd)


def _dn_scan_geometry(ln):
    tb = _pick(ln, (640, 256))
    return tb, ln // tb, tb // 64


def _dn_scan_fwd_call(u, w, qg, kg, attn, egl):
    nh, ln, hd = u.shape
    tb, nblk, nck = _dn_scan_geometry(ln)

    def body(u_ref, w_ref, qg_ref, kg_ref, a_ref, e_ref, o_ref, hist_ref, s_ref):
        @pl.when(pl.program_id(0) == 0)
        def _():
            s_ref[...] = jnp.zeros_like(s_ref)

        def chunk(ci, _):
            rows = pl.ds(pl.multiple_of(ci * 64, 64), 64)
            for h in range(nh):
                s = s_ref[h]
                hist_ref[ci, h] = s
                v_new = u_ref[h, rows, :] - _dot16(w_ref[h, rows, :], s)
                o_ref[h, rows, :] = _dot16(qg_ref[h, rows, :], s) + _dot16(a_ref[h, rows, :], v_new)
                s_ref[h] = s * e_ref[h, rows, :][0:1, :] + _dot16(kg_ref[h, rows, :], v_new, "tn")
            return 0

        lax.fori_loop(0, nck, chunk, 0)

    blk = pl.BlockSpec((nh, tb, hd), lambda i: (0, i, 0))
    return pl.pallas_call(
        body, grid=(nblk,), in_specs=[blk] * 6,
        out_specs=[blk, pl.BlockSpec((nck, nh, hd, hd), lambda i: (i, 0, 0, 0))],
        out_shape=[SDS((nh, ln, hd), f32), SDS((ln // 64, nh, hd, hd), f32)],
        scratch_shapes=[pltpu.VMEM((nh, hd, hd), f32)], name="dn_scan_fwd", compiler_params=_cparams())(u, w, qg, kg, attn, egl)


def _dn_scan_bwd_call(u, w, qg, kg, attn, egl, hist, do):
    nh, ln, hd = u.shape
    tb, nblk, nck = _dn_scan_geometry(ln)

    def body(u_ref, w_ref, qg_ref, kg_ref, a_ref, e_ref, hist_ref, do_ref,
             du_ref, dw_ref, dqg_ref, dkg_ref, da_ref, de_ref, ds_ref):
        @pl.when(pl.program_id(0) == 0)
        def _():
            ds_ref[...] = jnp.zeros_like(ds_ref)

        row0 = lax.broadcasted_iota(jnp.int32, (64, hd), 0) == 0

        def chunk(i, _):
            ci = nck - 1 - i
            rows = pl.ds(pl.multiple_of(ci * 64, 64), 64)
            for h in range(nh):
                s = hist_ref[ci, h]
                ds = ds_ref[h]
                dov = do_ref[h, rows, :]
                wv, kgv, av = w_ref[h, rows, :], kg_ref[h, rows, :], a_ref[h, rows, :]
                egl_row = e_ref[h, rows, :][0:1, :]
                v_new = u_ref[h, rows, :] - _dot16(wv, s)
                dv_new = _dot16(av, dov, "tn") + _dot16(kgv, ds)
                du_ref[h, rows, :] = dv_new
                dw_ref[h, rows, :] = -_dot16(dv_new, s, "nt")
                dqg_ref[h, rows, :] = _dot16(dov, s, "nt")
                dkg_ref[h, rows, :] = _dot16(v_new, ds, "nt")
                da_ref[h, rows, :] = _dot16(dov, v_new, "nt")
                de_ref[h, rows, :] = jnp.where(row0, jnp.sum(s * ds, axis=0, keepdims=True), 0.0)
                ds_ref[h] = ds * egl_row + _dot16(qg_ref[h, rows, :], dov, "tn") - _dot16(wv, dv_new, "tn")
            return 0

        lax.fori_loop(0, nck, chunk, 0)

    blk = pl.BlockSpec((nh, tb, hd), lambda i: (0, nblk - 1 - i, 0))
    hblk = pl.BlockSpec((nck, nh, hd, hd), lambda i: (nblk - 1 - i, 0, 0, 0))
    return pl.pallas_call(
        body, grid=(nblk,), in_specs=[blk] * 6 + [hblk, blk], out_specs=[blk] * 6,
        out_shape=[SDS((nh, ln, hd), f32)] * 6, scratch_shapes=[pltpu.VMEM((nh, hd, hd), f32)],
        name="dn_scan_bwd", compiler_params=_cparams())(u, w, qg, kg, attn, egl, hist, do)


@jax.custom_vjp
def dn_scan(u, w, qg, kg, attn, egl):
    return _dn_scan_fwd_call(u, w, qg, kg, attn, egl)[0]


def _dn_scan_fwd(u, w, qg, kg, attn, egl):
    o, hist = _dn_scan_fwd_call(u, w, qg, kg, attn, egl)
    return o, (u, w, qg, kg, attn, egl, hist)


def _dn_scan_bwd(res, do):
    return tuple(_dn_scan_bwd_call(*res, do))


dn_scan.defvjp(_dn_scan_fwd, _dn_scan_bwd)


def _cmul(a, b):
    return a[0] * b[0] - a[1] * b[1], a[0] * b[1] + a[1] * b[0]


def _powers(a1):
    a2 = _cmul(a1, a1)
    a3 = _cmul(a2, a1)
    a4 = _cmul(a2, a2)
    return [a1, a2, a3, a4, _cmul(a4, a1), _cmul(a4, a2), _cmul(a4, a3), _cmul(a4, a4)]


def _row_table(pw, row, order):
    tr = jnp.zeros(row.shape, f32)
    ti = jnp.zeros(row.shape, f32)
    for r in range(8):
        p = pw[order(r)]
        tr = tr + jnp.where(row == r, p[0], 0.0)
        ti = ti + jnp.where(row == r, p[1], 0.0)
    return tr, ti


def _s5_fwd_call(u, bmat, cmat, are, aim):
    ln, wu = u.shape
    w2 = bmat.shape[1]
    nl = w2 // 2
    tm = _pick(ln, (256, 128))

    def body(u_ref, b_ref, c_ref, are_ref, aim_ref, st_ref, y_ref, bu_ref, cr_ref, ci_ref):
        @pl.when(pl.program_id(0) == 0)
        def _():
            cr_ref[...] = jnp.zeros_like(cr_ref)
            ci_ref[...] = jnp.zeros_like(ci_ref)

        bu_ref[...] = _dot16(u_ref[...], b_ref[...])
        pw = _powers((are_ref[...], aim_ref[...]))
        row = lax.broadcasted_iota(jnp.int32, (8, nl), 0)
        table = _row_table(pw, row, lambda r: r)

        def tile(t, c):
            rows = pl.ds(pl.multiple_of(t * 8, 8), 8)
            x = (bu_ref[rows, 0:nl], bu_ref[rows, nl:w2])
            for kk in (1, 2, 4):
                sh = (jnp.where(row >= kk, pltpu.roll(x[0], kk, 0), 0.0), jnp.where(row >= kk, pltpu.roll(x[1], kk, 0), 0.0))
                m = _cmul(pw[kk - 1], sh)
                x = (x[0] + m[0], x[1] + m[1])
            m = _cmul(table, c)
            x = (x[0] + m[0], x[1] + m[1])
            st_ref[rows, 0:nl] = x[0]
            st_ref[rows, nl:w2] = x[1]
            return (jnp.sum(jnp.where(row == 7, x[0], 0.0), axis=0, keepdims=True),
                    jnp.sum(jnp.where(row == 7, x[1], 0.0), axis=0, keepdims=True))

        c = lax.fori_loop(0, tm // 8, tile, (cr_ref[...], ci_ref[...]), unroll=4)
        cr_ref[...] = c[0]
        ci_ref[...] = c[1]
        y_ref[...] = _dot16(st_ref[...], c_ref[...])

    vec = pl.BlockSpec((1, nl), lambda i: (0, 0))
    whole = lambda a: pl.BlockSpec(a.shape, lambda i: (0, 0))
    rows = lambda c: pl.BlockSpec((tm, c), lambda i: (i, 0))
    return pl.pallas_call(
        body, grid=(ln // tm,), in_specs=[rows(wu), whole(bmat), whole(cmat), vec, vec], out_specs=[rows(w2), rows(wu)],
        out_shape=[SDS((ln, w2), f32), SDS((ln, wu), f32)],
        scratch_shapes=[pltpu.VMEM((tm, w2), f32), pltpu.VMEM((1, nl), f32), pltpu.VMEM((1, nl), f32)],
        name="s5_fwd", compiler_params=_cparams())(u, bmat, cmat, are, aim)


def _s5_bwd_call(u, bmat, cmat, are, aim, st, dy):
    ln, wu = u.shape
    w2 = bmat.shape[1]
    nl = w2 // 2
    tm = _pick(ln, (256, 128))
    n = ln // tm
    t8 = tm // 8

    def body(u_ref, st_ref, halo_ref, dy_ref, b_hbm, c_hbm, are_ref, aim_ref, du_ref, db_hbm, dc_hbm, dar_ref, dai_ref,
             b_ref, c_ref, db_ref, dc_ref, sb_ref, d_ref, cr_ref, ci_ref):
        i = pl.program_id(0)
        g_ref = d_ref

        @pl.when(i == 0)
        def _():
            pltpu.sync_copy(b_hbm, b_ref)
            pltpu.sync_copy(c_hbm, c_ref)
            db_ref[...] = jnp.zeros_like(db_ref)
            dc_ref[...] = jnp.zeros_like(dc_ref)
            cr_ref[...] = jnp.zeros_like(cr_ref)
            ci_ref[...] = jnp.zeros_like(ci_ref)
            dar_ref[...] = jnp.zeros_like(dar_ref)
            dai_ref[...] = jnp.zeros_like(dai_ref)

        d_ref[...] = _dot16(dy_ref[...], c_ref[...], "nt")
        sb_ref[0:8, :] = jnp.where(i < n - 1, halo_ref[...], 0.0)
        sb_ref[8:tm + 8, :] = st_ref[...]
        pw = _powers((are_ref[...], -aim_ref[...]))
        row = lax.broadcasted_iota(jnp.int32, (8, nl), 0)
        table = _row_table(pw, row, lambda r: 7 - r)

        def tile(j, carry):
            c, acc_r, acc_i = carry
            t = t8 - 1 - j
            rows = pl.ds(pl.multiple_of(t * 8, 8), 8)
            x = (d_ref[rows, 0:nl], d_ref[rows, nl:w2])
            for kk in (1, 2, 4):
                sh = (jnp.where(row < 8 - kk, pltpu.roll(x[0], 8 - kk, 0), 0.0),
                      jnp.where(row < 8 - kk, pltpu.roll(x[1], 8 - kk, 0), 0.0))
                m = _cmul(pw[kk - 1], sh)
                x = (x[0] + m[0], x[1] + m[1])
            m = _cmul(table, c)
            x = (x[0] + m[0], x[1] + m[1])
            g_ref[rows, 0:nl] = x[0]
            g_ref[rows, nl:w2] = x[1]
            prev = (sb_ref[rows, 0:nl], sb_ref[rows, nl:w2])
            cur = (st_ref[rows, 0:nl], st_ref[rows, nl:w2])
            last_r = jnp.sum(jnp.where(row == 7, prev[0], 0.0), axis=0, keepdims=True)
            last_i = jnp.sum(jnp.where(row == 7, prev[1], 0.0), axis=0, keepdims=True)
            sp = (jnp.where(row == 0, last_r, pltpu.roll(cur[0], 1, 0)), jnp.where(row == 0, last_i, pltpu.roll(cur[1], 1, 0)))
            acc_r = acc_r + x[0] * sp[0] + x[1] * sp[1]
            acc_i = acc_i + x[1] * sp[0] - x[0] * sp[1]
            c = (jnp.sum(jnp.where(row == 0, x[0], 0.0), axis=0, keepdims=True),
                 jnp.sum(jnp.where(row == 0, x[1], 0.0), axis=0, keepdims=True))
            return c, acc_r, acc_i

        z8 = jnp.zeros((8, nl), f32)
        c, acc_r, acc_i = lax.fori_loop(0, t8, tile, ((cr_ref[...], ci_ref[...]), z8, z8), unroll=4)
        cr_ref[...] = c[0]
        ci_ref[...] = c[1]
        dar_ref[...] += jnp.sum(acc_r, axis=0, keepdims=True)
        dai_ref[...] += jnp.sum(acc_i, axis=0, keepdims=True)
        g = g_ref[...].astype(bf16)
        du_ref[...] = lax.dot_general(g, b_ref[...], _DIMS["nt"], preferred_element_type=f32)
        db_ref[...] += lax.dot_general(u_ref[...].astype(bf16), g, _DIMS["tn"], preferred_element_type=f32)
        dc_ref[...] += _dot16(st_ref[...], dy_ref[...], "tn")

        @pl.when(i == n - 1)
        def _():
            pltpu.sync_copy(db_ref, db_hbm)
            pltpu.sync_copy(dc_ref, dc_hbm)

    vec = pl.BlockSpec((1, nl), lambda i: (0, 0))
    rows = lambda c: pl.BlockSpec((tm, c), lambda i: (n - 1 - i, 0))
    halo = pl.BlockSpec((8, w2), lambda i: (jnp.maximum((n - 1 - i) * t8 - 1, 0), 0))
    anyspec = pl.BlockSpec(memory_space=pl.ANY)
    return pl.pallas_call(
        body, grid=(n,), in_specs=[rows(wu), rows(w2), halo, rows(wu), anyspec, anyspec, vec, vec],
        out_specs=[rows(wu), anyspec, anyspec, vec, vec],
        out_shape=[SDS((ln, wu), f32), SDS(bmat.shape, f32), SDS(cmat.shape, f32), SDS((1, nl), f32), SDS((1, nl), f32)],
        scratch_shapes=[pltpu.VMEM(bmat.shape, bf16), pltpu.VMEM(cmat.shape, bf16), pltpu.VMEM(bmat.shape, f32),
                        pltpu.VMEM(cmat.shape, f32), pltpu.VMEM((tm + 8, w2), f32), pltpu.VMEM((tm, w2), f32),
                        pltpu.VMEM((1, nl), f32), pltpu.VMEM((1, nl), f32)],
        name="s5_bwd", compiler_params=_cparams())(u, st, st, dy, bmat, cmat, are, aim)


@jax.custom_vjp
def s5_core(u, bmat, cmat, are, aim):
    return _s5_fwd_call(u, bmat.astype(bf16), cmat.astype(bf16), are, aim)[1]


def _s5_core_fwd(u, bmat, cmat, are, aim):
    bb, cb = bmat.astype(bf16), cmat.astype(bf16)
    st, y = _s5_fwd_call(u, bb, cb, are, aim)
    return y, (u, bb, cb, are, aim, st)


def _s5_core_bwd(res, dy):
    return tuple(_s5_bwd_call(*res, dy))


s5_core.defvjp(_s5_core_fwd, _s5_core_bwd)


def _final_call(h, target, gnorm):
    ln, d = h.shape
    tm = PAD + N_META
    skip = (ln - target.shape[0]) // tm

    def body(h_ref, t_ref, g_ref, loss_ref, dh_ref, dg_ref):
        i = pl.program_id(0)

        @pl.when(i == 0)
        def _():
            loss_ref[...] = jnp.zeros_like(loss_ref)
            dg_ref[...] = jnp.zeros_like(dg_ref)

        live = (i >= skip).astype(f32)
        y, vjp = jax.vjp(_rms, h_ref[...], g_ref[...])
        err = (y - t_ref[...]) * live
        loss_ref[...] += 0.5 * jnp.sum(jnp.sum(err * err, axis=1, keepdims=True) / d, axis=0, keepdims=True)
        dh, dg = vjp(err / d)
        dh_ref[...] = dh
        dg_ref[...] += dg

    return pl.pallas_call(
        body, grid=(ln // tm,),
        in_specs=[pl.BlockSpec((tm, d), lambda i: (i, 0)), pl.BlockSpec((tm, d), lambda i: (jnp.maximum(i - skip, 0), 0)),
                  pl.BlockSpec((1, d), lambda i: (0, 0))],
        out_specs=[pl.BlockSpec((1, 1), lambda i: (0, 0)), pl.BlockSpec((tm, d), lambda i: (i, 0)), pl.BlockSpec((1, d), lambda i: (0, 0))],
        out_shape=[SDS((1, 1), f32), SDS((ln, d), f32), SDS((1, d), f32)], name="final_loss", compiler_params=_cparams())(h, target, gnorm)


def _exchange(x, gather, name):
    r, c = x.shape[-2:]

    def body(x_ref, o_ref, send_sems, recv_sems, local_sem):
        ax, ay, ac = lax.axis_index("x"), lax.axis_index("y"), lax.axis_index("c")
        me = 4 * ax + 2 * ay + ac
        local = pltpu.make_async_copy(x_ref if gather else x_ref.at[me], o_ref.at[me], local_sem)
        local.start()
        sent = []
        for k in range(1, N_DEV):
            px = 1 - ax if k & 4 else ax
            py = 1 - ay if k & 2 else ay
            pc = 1 - ac if k & 1 else ac
            p = 4 * px + 2 * py + pc
            cp = pltpu.make_async_remote_copy(
                src_ref=x_ref if gather else x_ref.at[p], dst_ref=o_ref.at[me], send_sem=send_sems.at[k - 1],
                recv_sem=recv_sems.at[k - 1], device_id=(px, py, pc), device_id_type=pl.DeviceIdType.MESH)
            cp.start()
            sent.append((cp, p, (px, py, pc)))
        for k, (cp, p, peer) in enumerate(sent):
            pltpu.make_async_remote_copy(
                src_ref=o_ref.at[p], dst_ref=o_ref.at[p], send_sem=send_sems.at[k], recv_sem=recv_sems.at[k],
                device_id=peer, device_id_type=pl.DeviceIdType.MESH).wait_recv()
        for cp, _, _ in sent:
            cp.wait_send()
        local.wait()

    hbm = pl.BlockSpec(memory_space=pltpu.HBM)
    return pl.pallas_call(
        body, in_specs=[hbm], out_specs=hbm, out_shape=SDS((N_DEV, r, c), x.dtype),
        scratch_shapes=[pltpu.SemaphoreType.DMA((N_DEV - 1,)), pltpu.SemaphoreType.DMA((N_DEV - 1,)), pltpu.SemaphoreType.DMA],
        name=name)(x)


def _gather_via_sibling(x, name):
    r, c = x.shape

    def body(x_ref, o_ref, send_sems, recv_sems, local_sem):
        ax, ay, ac = lax.axis_index("x"), lax.axis_index("y"), lax.axis_index("c")
        me, sibling = (ax, ay, ac), (ax, ay, 1 - ac)
        chips = [(1 - ax, ay), (ax, 1 - ay), (1 - ax, 1 - ay)]

        def slot(px, py, pc):
            return o_ref.at[4 * px + 2 * py + pc]

        def copy(k, block, to, src=None):
            return pltpu.make_async_remote_copy(
                src_ref=slot(*block) if src is None else src, dst_ref=slot(*block), send_sem=send_sems.at[k],
                recv_sem=recv_sems.at[k], device_id=to, device_id_type=pl.DeviceIdType.MESH)

        mine = pltpu.make_async_copy(x_ref, slot(*me), local_sem)
        mine.start()
        first = [copy(0, me, sibling, src=x_ref)] + [copy(1 + j, me, (*chip, ac), src=x_ref) for j, chip in enumerate(chips)]
        for cp in first:
            cp.start()
        passed = [copy(4 + j, (*chip, ac), sibling) for j, chip in enumerate(chips)]
        for j, chip in enumerate(chips):
            copy(1 + j, (*chip, ac), me).wait_recv()
            passed[j].start()
        copy(0, sibling, me).wait_recv()
        for j, chip in enumerate(chips):
            copy(4 + j, (*chip, 1 - ac), me).wait_recv()
        for cp in first + passed:
            cp.wait_send()
        mine.wait()

    hbm = pl.BlockSpec(memory_space=pltpu.HBM)
    return pl.pallas_call(
        body, in_specs=[hbm], out_specs=hbm, out_shape=SDS((N_DEV, r, c), x.dtype),
        scratch_shapes=[pltpu.SemaphoreType.DMA((N_DEV - 1,)), pltpu.SemaphoreType.DMA((N_DEV - 1,)), pltpu.SemaphoreType.DMA],
        name=name)(x)


def _sum_slots(x, name):
    _, r, c = x.shape
    tr = _pick(r, (PACK_BLOCK_ROWS, 256, 128, 64, 32, 16, 8))

    def body(x_ref, o_ref):
        acc = x_ref[0].astype(f32)
        for p in range(1, N_DEV):
            acc = acc + x_ref[p].astype(f32)
        o_ref[...] = acc

    return pl.pallas_call(body, grid=(r // tr,), in_specs=[pl.BlockSpec((N_DEV, tr, c), lambda i: (0, i, 0))],
                          out_specs=pl.BlockSpec((tr, c), lambda i: (i, 0)), out_shape=SDS((r, c), f32), name=name,
                          compiler_params=_cparams())(x)


def _adamw(w, g, m, v, name):
    shape = w.shape
    w, g, m, v = (t.reshape((-1, shape[-1]) if t.ndim > 1 else (1, -1)) for t in (w, g, m, v))
    r, c = w.shape
    tr = _pick(r, (512, 256, 128, 64, 32, 16, 8)) if r % 8 == 0 else r

    def body(w_ref, g_ref, m_ref, v_ref, d_ref, mo_ref, vo_ref):
        gv = g_ref[...]
        mn = ADAM_B1 * m_ref[...] + (1.0 - ADAM_B1) * gv
        vn = ADAM_B2 * v_ref[...] + (1.0 - ADAM_B2) * jnp.square(gv)
        m_hat = mn / (1.0 - ADAM_B1 ** ADAM_STEP)
        v_hat = vn / (1.0 - ADAM_B2 ** ADAM_STEP)
        d_ref[...] = -ADAM_LR * (m_hat / (jnp.sqrt(v_hat) + ADAM_EPS) + ADAM_WD * w_ref[...])
        mo_ref[...] = mn
        vo_ref[...] = vn

    blk = pl.BlockSpec((tr, c), lambda i: (i, 0))
    outs = pl.pallas_call(body, grid=(r // tr,), in_specs=[blk] * 4, out_specs=[blk] * 3, out_shape=[SDS((r, c), f32)] * 3,
                          name=name, compiler_params=_cparams())(w, g, m, v)
    return [t.reshape(shape) for t in outs]


_WEIGHTS = [
    ("meta_tokens", (N_META, D_MODEL), 1), ("ffn1_norm", (DEPTH, D_MODEL), None),
    ("ffn1_w_gate", (DEPTH, D_MODEL, D_FF), 2), ("ffn1_w_up", (DEPTH, D_MODEL, D_FF), 2),
    ("ffn1_w_down", (DEPTH, D_FF, D_MODEL), 1), ("mix_norm", (DEPTH, D_MODEL), None),
    ("w_in", (DEPTH, D_MODEL, IN_WIDTH), 2), ("sb_out_norm", (DEPTH, HEAD_DIM), None),
    ("dn_conv_w", (DEPTH, DN_CONV, 3 * GW), 2), ("dn_a_log", (DEPTH, N_HEADS), None),
    ("dn_dt_bias", (DEPTH, N_HEADS), None), ("dn_out_norm", (DEPTH, HEAD_DIM), None),
    ("s5_a_re", (DEPTH, S5_GROUPS, S5_STATE), None), ("s5_a_im", (DEPTH, S5_GROUPS, S5_STATE), None),
    ("s5_log_dt", (DEPTH, S5_GROUPS), None), ("s5_b_re", (DEPTH, S5_GROUPS, S5_STATE, S5_GROUP), None),
    ("s5_b_im", (DEPTH, S5_GROUPS, S5_STATE, S5_GROUP), None), ("s5_c_re", (DEPTH, S5_GROUPS, S5_GROUP, S5_STATE), None),
    ("s5_c_im", (DEPTH, S5_GROUPS, S5_GROUP, S5_STATE), None), ("s5_d", (DEPTH, S5_WIDTH), None),
    ("s5_w_glu", (DEPTH, S5_WIDTH, S5_WIDTH), 1), ("s5_b_glu", (DEPTH, S5_WIDTH), None),
    ("s5_out_norm", (DEPTH, S5_WIDTH), None), ("w_out", (DEPTH, D_MODEL, D_MODEL), 1),
    ("ffn2_norm", (DEPTH, D_MODEL), None), ("ffn2_w_gate", (DEPTH, D_MODEL, D_FF), 2),
    ("ffn2_w_up", (DEPTH, D_MODEL, D_FF), 2), ("ffn2_w_down", (DEPTH, D_FF, D_MODEL), 1),
    ("final_norm", (D_MODEL,), None),
]
_SHARDED = [(n, s, a) for n, s, a in _WEIGHTS if a is not None]
_REPL = [(n, s) for n, s, a in _WEIGHTS if a is None]
PACK_ROW_ALIGN = 16
PACK_BLOCK_ROWS = 256


def _shard_shape(shape, axis):
    return tuple(d // N_DEV if i == axis else d for i, d in enumerate(shape))


def _pack_rows_of(n):
    rows = -(-n // PACK_COLS)
    return -(-rows // PACK_ROW_ALIGN) * PACK_ROW_ALIGN


def _as_rows(t, lead):
    head = t.shape[:lead]
    n = math.prod(t.shape[lead:])
    rows = _pack_rows_of(n)
    if n % PACK_COLS == 0:
        t = t.reshape(head + (n // PACK_COLS, PACK_COLS))
        return jnp.pad(t, [(0, 0)] * lead + [(0, rows - n // PACK_COLS), (0, 0)])
    t = jnp.pad(t.reshape(head + (n,)), [(0, 0)] * lead + [(0, rows * PACK_COLS - n)])
    return t.reshape(head + (rows, PACK_COLS))


def _pack(parts, lead=0):
    rows = [_as_rows(p, lead) for p in parts]
    total = sum(r.shape[lead] for r in rows)
    fill = -total % PACK_BLOCK_ROWS
    if fill:
        rows.append(jnp.zeros(rows[0].shape[:lead] + (fill, PACK_COLS), rows[0].dtype))
    return jnp.concatenate(rows, axis=lead)


def _unpack(pack, shapes, lead=0):
    out, off = [], 0
    head = pack.shape[:lead]
    for s in shapes:
        n = math.prod(s)
        rows = _pack_rows_of(n)
        blk = lax.slice_in_dim(pack, off, off + rows, axis=lead)
        if n % PACK_COLS == 0:
            out.append(lax.slice_in_dim(blk, 0, n // PACK_COLS, axis=lead).reshape(head + tuple(s)))
        else:
            out.append(blk.reshape(head + (rows * PACK_COLS,))[..., :n].reshape(head + tuple(s)))
        off += rows
    return out


def _unpack_gathered(g):
    blocks = _unpack(g, [_shard_shape(s, a) for _, s, a in _SHARDED], lead=1)
    return {name: jnp.moveaxis(blk.astype(f32), 0, axis).reshape(shape)
            for (name, shape, axis), blk in zip(_SHARDED, blocks)}


def _pack_by_dest(grads):
    parts = []
    for name, shape, axis in _SHARDED:
        ss = _shard_shape(shape, axis)
        g = grads[name].reshape(shape[:axis] + (N_DEV, ss[axis]) + shape[axis + 1:])
        parts.append(jnp.moveaxis(g, axis, 0))
    return _pack(parts, lead=1)


_rmsnorm_op = _make_blockop(_f_rmsnorm, "rmsnorm", (D_MODEL, D_MODEL), (640, 256, 128), (bf16, f32))
_swiglu_op = _make_blockop(_f_swiglu, "swiglu", (D_FF,), (256, 128), (bf16,))
_headnorm_op = _make_blockop(_f_headnorm, "sb_headnorm", (GW,), (640, 256, 128))
_dn_prep_op = _make_blockop(_f_dn_prep, "dn_prep", (GW,) * 5, (128,))
_dn_intra_op = _make_blockop(_f_dn_intra_heads, "dn_intra", (HEAD_DIM,) * 6, (256,), whole_lead=True)
_dn_out_op = _make_blockop(_f_dn_out, "dn_out", (GW,), (640, 256, 128))
_s5_param_op = _make_blockop(_f_s5_param, "s5_param", (1, 1, S5_GROUP, S5_GROUP), (256,))
_s5_post_op = _make_blockop(_f_s5_post, "s5_post", (S5_WIDTH,), (256, 128))
_lin_gu, _ = _make_linear("ffn_gu", out_dtype=bf16)
_, _lin_down_res = _make_linear("ffn_down", scale=0.5)
_lin_in, _ = _make_linear("mix_in")
_, _lin_out_res = _make_linear("mix_out")


def _heads(t):
    return jnp.transpose(t.reshape(t.shape[0], N_HEADS, HEAD_DIM), (1, 0, 2))


def _unheads(t):
    return jnp.transpose(t, (1, 0, 2)).reshape(t.shape[1], GW)


def _ffn(h, gnorm, wg, wu, wd):
    xn, h_skip = _rmsnorm_op((h,), (gnorm[None],))
    gu = _lin_gu(xn, jnp.concatenate([wg, wu], axis=1))
    (a,) = _swiglu_op((gu,), ())
    return _lin_down_res(a, wd, h_skip)


def _lane_row(vals, start):
    return jnp.pad(vals, (start, 128 - start - vals.shape[0]))[None]


def _block_diag(t):
    g, a, b = t.shape
    eye = jnp.eye(g, dtype=t.dtype)
    return (t[:, :, None, :] * eye[:, None, :, None]).reshape(g * a, g * b)


def _mixer(h, p):
    xn, h_skip = _rmsnorm_op((h,), (p["mix_norm"][None],))
    w_in = p["w_in"]
    w_pad = jnp.concatenate([w_in[:, :IN_SMALL], jnp.zeros((D_MODEL, C_S5U - IN_SMALL), f32), w_in[:, IN_SMALL:],
                             jnp.zeros((D_MODEL, IN_PAD - C_S5U - S5_WIDTH), f32)], axis=1)
    proj = _lin_in(xn, w_pad)
    o_sb = sb_attention(_heads(proj[:, C_SBQ:C_SBK]), _heads(proj[:, C_SBK:C_SBV]), _heads(proj[:, C_SBV:C_DNQKV]))
    (o_sb,) = _headnorm_op((_unheads(o_sb),), (jnp.tile(p["sb_out_norm"], N_HEADS)[None],))
    conv = conv_op(proj[:, C_DNQKV:C_DNZ], jnp.pad(p["dn_conv_w"], ((0, 8 - DN_CONV), (0, 0))))
    q, k, v, gc, bb = _dn_prep_op((conv, proj[:, C_DNBA:C_S5U]),
                                  (_lane_row(p["dn_a_log"], N_HEADS), _lane_row(p["dn_dt_bias"], N_HEADS)))
    parts = _dn_intra_op(tuple(_heads(t) for t in (q, k, v, gc, bb)), ())
    o_dn = _unheads(dn_scan(*parts))
    (o_dn,) = _dn_out_op((o_dn, proj[:, C_DNZ:C_DNBA]), (jnp.tile(p["dn_out_norm"], N_HEADS)[None],))
    u = proj[:, C_S5U:C_S5U + S5_WIDTH]
    col = lambda t: t.reshape(S5_LANES, 1)
    abr, abi, bbr, bbi = _s5_param_op(
        (col(p["s5_a_re"]), col(p["s5_a_im"]), col(jnp.repeat(p["s5_log_dt"], S5_STATE)),
         p["s5_b_re"].reshape(S5_LANES, S5_GROUP), p["s5_b_im"].reshape(S5_LANES, S5_GROUP)), ())
    to_b = lambda t: _block_diag(jnp.transpose(t.reshape(S5_GROUPS, S5_STATE, S5_GROUP), (0, 2, 1)))
    to_c = lambda t: _block_diag(jnp.transpose(t, (0, 2, 1)))
    y = s5_core(u, jnp.concatenate([to_b(bbr), to_b(bbi)], axis=1),
                jnp.concatenate([to_c(p["s5_c_re"]), -to_c(p["s5_c_im"])], axis=0),
                abr.reshape(1, S5_LANES), abi.reshape(1, S5_LANES))
    (o_s5,) = _s5_post_op((y, u), (p["s5_d"][None], p["s5_w_glu"], p["s5_b_glu"][None], p["s5_out_norm"][None]))
    return _lin_out_res(jnp.concatenate([o_sb, o_dn, o_s5], axis=1), p["w_out"], h_skip)


def _trunk(x2d, w):
    h = jnp.concatenate([jnp.zeros((PAD, D_MODEL), f32), w["meta_tokens"], x2d], axis=0)
    for l in range(DEPTH):
        p = {k: v[l] for k, v in w.items() if k not in ("meta_tokens", "final_norm")}
        h = _ffn(h, p["ffn1_norm"], p["ffn1_w_gate"], p["ffn1_w_up"], p["ffn1_w_down"])
        h = _mixer(h, p)
        h = _ffn(h, p["ffn2_norm"], p["ffn2_w_gate"], p["ffn2_w_up"], p["ffn2_w_down"])
    return h


def _step(x, loss_target, w, m, v):
    s_names = [n for n, _, _ in _SHARDED]
    r_names = [n for n, _ in _REPL]
    s_shapes = [_shard_shape(s, a) for _, s, a in _SHARDED]
    r_shapes = [s for _, s in _REPL] + [(1,)]
    shard_pack = _pack([w[n] for n in s_names])
    gathered = _gather_via_sibling(shard_pack.astype(bf16), "gather_weights")
    full = _unpack_gathered(gathered)
    full.update({n: w[n] for n, _ in _REPL})
    trunk_w = {k: t for k, t in full.items() if k != "final_norm"}
    h, vjp = jax.vjp(_trunk, x[0], trunk_w)
    loss, dh, dgf = _final_call(h, loss_target[0], full["final_norm"][None])
    dx, dw = vjp(dh)
    dw["final_norm"] = dgf[0]
    g_shard = _sum_slots(_exchange(_pack_by_dest(dw).astype(bf16), False, "scatter_grads"), "sum_shard_grads")
    repl_pack = _pack([dw[n] for n in r_names] + [loss.reshape(1)])
    g_repl = _sum_slots(_exchange(repl_pack, True, "gather_small_grads"), "sum_small_grads")
    outs = {"grad_" + n: t for n, t in zip(s_names, _unpack(g_shard, s_shapes))}
    outs.update({"grad_" + n: t for n, t in zip(r_names + ["loss"], _unpack(g_repl, r_shapes))})
    loss_total = outs["grad_loss"][0]
    names = [n for n, _, _ in _WEIGHTS]
    for n in names:
        outs["delta_" + n], outs["new_m_" + n], outs["new_v_" + n] = _adamw(w[n], outs["grad_" + n], m[n], v[n], "adamw_" + n)
    return (loss_total, dx[None], *[outs["grad_" + n] for n in names], *[outs["delta_" + n] for n in names],
            *[outs["new_m_" + n] for n in names], *[outs["new_v_" + n] for n in names])


def kernel(x, meta_tokens, ffn1_norm, ffn1_w_gate, ffn1_w_up, ffn1_w_down, mix_norm, w_in, sb_out_norm, dn_conv_w, dn_a_log, dn_dt_bias, dn_out_norm, s5_a_re, s5_a_im, s5_log_dt, s5_b_re, s5_b_im, s5_c_re, s5_c_im, s5_d, s5_w_glu, s5_b_glu, s5_out_norm, w_out, ffn2_norm, ffn2_w_gate, ffn2_w_up, ffn2_w_down, final_norm, loss_target, m_meta_tokens, m_ffn1_norm, m_ffn1_w_gate, m_ffn1_w_up, m_ffn1_w_down, m_mix_norm, m_w_in, m_sb_out_norm, m_dn_conv_w, m_dn_a_log, m_dn_dt_bias, m_dn_out_norm, m_s5_a_re, m_s5_a_im, m_s5_log_dt, m_s5_b_re, m_s5_b_im, m_s5_c_re, m_s5_c_im, m_s5_d, m_s5_w_glu, m_s5_b_glu, m_s5_out_norm, m_w_out, m_ffn2_norm, m_ffn2_w_gate, m_ffn2_w_up, m_ffn2_w_down, m_final_norm, v_meta_tokens, v_ffn1_norm, v_ffn1_w_gate, v_ffn1_w_up, v_ffn1_w_down, v_mix_norm, v_w_in, v_sb_out_norm, v_dn_conv_w, v_dn_a_log, v_dn_dt_bias, v_dn_out_norm, v_s5_a_re, v_s5_a_im, v_s5_log_dt, v_s5_b_re, v_s5_b_im, v_s5_c_re, v_s5_c_im, v_s5_d, v_s5_w_glu, v_s5_b_glu, v_s5_out_norm, v_w_out, v_ffn2_norm, v_ffn2_w_gate, v_ffn2_w_up, v_ffn2_w_down, v_final_norm):
    names = [n for n, _, _ in _WEIGHTS]
    ws = (meta_tokens, ffn1_norm, ffn1_w_gate, ffn1_w_up, ffn1_w_down, mix_norm, w_in, sb_out_norm, dn_conv_w, dn_a_log, dn_dt_bias, dn_out_norm, s5_a_re, s5_a_im, s5_log_dt, s5_b_re, s5_b_im, s5_c_re, s5_c_im, s5_d, s5_w_glu, s5_b_glu, s5_out_norm, w_out, ffn2_norm, ffn2_w_gate, ffn2_w_up, ffn2_w_down, final_norm)
    ms = (m_meta_tokens, m_ffn1_norm, m_ffn1_w_gate, m_ffn1_w_up, m_ffn1_w_down, m_mix_norm, m_w_in, m_sb_out_norm, m_dn_conv_w, m_dn_a_log, m_dn_dt_bias, m_dn_out_norm, m_s5_a_re, m_s5_a_im, m_s5_log_dt, m_s5_b_re, m_s5_b_im, m_s5_c_re, m_s5_c_im, m_s5_d, m_s5_w_glu, m_s5_b_glu, m_s5_out_norm, m_w_out, m_ffn2_norm, m_ffn2_w_gate, m_ffn2_w_up, m_ffn2_w_down, m_final_norm)
    vs = (v_meta_tokens, v_ffn1_norm, v_ffn1_w_gate, v_ffn1_w_up, v_ffn1_w_down, v_mix_norm, v_w_in, v_sb_out_norm, v_dn_conv_w, v_dn_a_log, v_dn_dt_bias, v_dn_out_norm, v_s5_a_re, v_s5_a_im, v_s5_log_dt, v_s5_b_re, v_s5_b_im, v_s5_c_re, v_s5_c_im, v_s5_d, v_s5_w_glu, v_s5_b_glu, v_s5_out_norm, v_w_out, v_ffn2_norm, v_ffn2_w_gate, v_ffn2_w_up, v_ffn2_w_down, v_final_norm)
    return _step(x, loss_target, dict(zip(names, ws)), dict(zip(names, ms)), dict(zip(names, vs)))
```

```python
import functools
import math

import jax
import jax.numpy as jnp
from jax import lax
from jax.experimental import pallas as pl
from jax.experimental.pallas import tpu as pltpu

f32 = jnp.float32
bf16 = jnp.bfloat16
HI = lax.Precision.HIGHEST
SDS = jax.ShapeDtypeStruct

N_DEV = 8
D_MODEL = 1024
N_META = 16
PAD = 240
HEAD_DIM = 64
N_HEADS = 4
GW = N_HEADS * HEAD_DIM
DN_CONV = 4
S5_WIDTH = 512
S5_GROUP = 16
S5_GROUPS = 32
S5_STATE = 64
S5_LANES = S5_GROUPS * S5_STATE
D_FF = 2816
DEPTH = 2
EPS = 1e-6
C_SBQ, C_SBK, C_SBV, C_DNQKV, C_DNZ, C_DNBA, C_S5U, IN_PAD = 0, 256, 512, 768, 1536, 1792, 1920, 2560
IN_WIDTH = 2312
IN_SMALL = 1800
VMEM_LIMIT = 56 * 1024 * 1024
PACK_COLS = 512

ADAM_LR, ADAM_B1, ADAM_B2, ADAM_EPS, ADAM_WD, ADAM_STEP = 0.001, 0.9, 0.999, 1e-08, 0.01, 10


def _pick(n, cands):
    for c in cands:
        if n % c == 0:
            return c
    return n


def _cparams():
    return pltpu.CompilerParams(vmem_limit_bytes=VMEM_LIMIT)


_DIMS = {"nn": (((1,), (0,)), ((), ())), "nt": (((1,), (1,)), ((), ())), "tn": (((0,), (0,)), ((), ()))}


def _dot16(a, b, mode="nn"):
    return lax.dot_general(a.astype(bf16), b.astype(bf16), _DIMS[mode], preferred_element_type=f32)


def _make_bdot(mode):
    @jax.custom_vjp
    def f(a, b):
        return _dot16(a, b, mode)

    def fwd(a, b):
        return _dot16(a, b, mode), (a, b)

    def bwd(res, g):
        a, b = res
        if mode == "nn":
            return _dot16(g, b, "nt"), _dot16(a, g, "tn")
        if mode == "nt":
            return _dot16(g, b, "nn"), _dot16(g, a, "tn")
        return _dot16(b, g, "nt"), _dot16(a, g, "nn")

    f.defvjp(fwd, bwd)
    return f


bdot = _make_bdot("nn")
bdot_nt = _make_bdot("nt")
bdot_tn = _make_bdot("tn")


def _split3(x):
    h = x.astype(bf16)
    r = x - h.astype(f32)
    m = r.astype(bf16)
    l = (r - m.astype(f32)).astype(bf16)
    return h, m, l


def _dot3(a, b, mode="nn"):
    ah, am, al = _split3(a)
    bh, bm, bl = _split3(b)
    d = lambda x, y: lax.dot_general(x, y, _DIMS[mode], preferred_element_type=f32)
    return ((d(al, bh) + d(ah, bl)) + d(am, bm)) + ((d(am, bh) + d(ah, bm)) + d(ah, bh))


def _make_xdot(dot):
    @jax.custom_vjp
    def f(a, b):
        return dot(a, b, "nn")

    def fwd(a, b):
        return dot(a, b, "nn"), (a, b)

    def bwd(res, g):
        a, b = res
        return dot(g, b, "nt"), dot(a, g, "tn")

    f.defvjp(fwd, bwd)
    return f


xdot = _make_xdot(_dot3)


def _sel_dot(sel, x, mode):
    s = sel.astype(bf16)
    h, m, l = _split3(x)
    d = lambda y: lax.dot_general(s, y, _DIMS[mode], preferred_element_type=f32)
    return (d(l) + d(m)) + d(h)


@jax.custom_vjp
def _select_rows(sel, x):
    return _sel_dot(sel, x, "nn")


def _select_rows_fwd(sel, x):
    return _sel_dot(sel, x, "nn"), sel


def _select_rows_bwd(sel, g):
    return jnp.zeros_like(sel), _sel_dot(sel, g, "tn")


_select_rows.defvjp(_select_rows_fwd, _select_rows_bwd)


def _matmul(a, b, mode, name, out_dtype=f32, scale=None, res=None):
    row_c = (1280, 640, 512, 384, 256, 128)
    col_c = (1408, 1280, 1024, 512, 256, 128)
    if mode == "nn":
        (m, k), n = a.shape, b.shape[1]
        bo1, bo2, br = _pick(m, row_c), _pick(n, col_c), _pick(k, (1024, 1408, 1280, 512, 256, 128))
        out, red = (m, n), k
        a_spec = pl.BlockSpec((bo1, br), lambda i, j, r: (i, r))
        b_spec = pl.BlockSpec((br, bo2), lambda i, j, r: (r, j))
    elif mode == "nt":
        (m, n), k = a.shape, b.shape[0]
        bo1, bo2, br = _pick(m, row_c), _pick(k, col_c), _pick(n, (1408, 1280, 1024, 512, 256, 128))
        out, red = (m, k), n
        a_spec = pl.BlockSpec((bo1, br), lambda i, j, r: (i, r))
        b_spec = pl.BlockSpec((bo2, br), lambda i, j, r: (j, r))
    else:
        (m, k), n = a.shape, b.shape[1]
        bo1, bo2, br = _pick(k, (1024, 1408, 1280, 512, 256, 128)), _pick(n, col_c), _pick(m, row_c[1:])
        out, red = (k, n), m
        a_spec = pl.BlockSpec((br, bo1), lambda i, j, r: (r, i))
        b_spec = pl.BlockSpec((br, bo2), lambda i, j, r: (r, j))
    nred = red // br
    o_spec = pl.BlockSpec((bo1, bo2), lambda i, j, r: (i, j))

    def body(a_ref, b_ref, *rest):
        res_ref = rest[0] if res is not None else None
        o_ref, acc_ref = rest[-2:]
        r = pl.program_id(2)

        @pl.when(r == 0)
        def _():
            acc_ref[...] = jnp.zeros_like(acc_ref)

        acc_ref[...] += _dot16(a_ref[...], b_ref[...], mode)

        @pl.when(r == nred - 1)
        def _():
            y = acc_ref[...]
            if scale is not None:
                y = y * scale
            if res_ref is not None:
                y = y + res_ref[...].astype(f32)
            o_ref[...] = y.astype(out_dtype)

    operands = (a, b) if res is None else (a, b, res)
    return pl.pallas_call(
        body, grid=(out[0] // bo1, out[1] // bo2, nred), in_specs=[a_spec, b_spec] + ([o_spec] if res is not None else []),
        out_specs=o_spec, out_shape=SDS(out, out_dtype),
        scratch_shapes=[pltpu.VMEM((bo1, bo2), f32)], name=name, compiler_params=_cparams())(*operands)


def _make_linear(name, out_dtype=f32, scale=None):
    def run(x, w, h):
        return _matmul(x, w.astype(bf16), "nn", name + "_fwd", out_dtype=out_dtype, scale=scale, res=h)

    def grads(x, w, dy):
        dx = _matmul(dy, w.astype(bf16), "nt", name + "_dx", out_dtype=x.dtype, scale=scale)
        return dx, _matmul(x, dy, "tn", name + "_dw", scale=scale)

    @jax.custom_vjp
    def lin(x, w):
        return run(x, w, None)

    lin.defvjp(lambda x, w: (run(x, w, None), (x, w)), lambda res, dy: grads(*res, dy))

    @jax.custom_vjp
    def lin_res(x, w, h):
        return run(x, w, h)

    lin_res.defvjp(lambda x, w, h: (run(x, w, h), (x, w)), lambda res, dy: (*grads(*res, dy), dy))
    return lin, lin_res


def _make_blockop(f, name, out_cols, tm_cands, out_dtypes=None, whole_lead=False):
    out_dtypes = out_dtypes or (f32,) * len(out_cols)

    def specs(arrs, tm, lead, g=None):
        cols = [a if isinstance(a, int) else a.shape[-1] for a in arrs]
        if g is not None:
            return [pl.BlockSpec((g, tm, c), lambda i: (0, i, 0)) for c in cols]
        if lead:
            return [pl.BlockSpec((None, tm, c), lambda g, i: (g, i, 0)) for c in cols]
        return [pl.BlockSpec((tm, c), lambda i: (i, 0)) for c in cols]

    def pspecs(params, lead):
        if lead:
            return [pl.BlockSpec(p.shape, lambda g, i: (0, 0)) for p in params]
        return [pl.BlockSpec(p.shape, lambda i: (0, 0)) for p in params]

    def geometry(ins):
        lead = ins[0].ndim == 3 and not whole_lead
        g = ins[0].shape[0] if ins[0].ndim == 3 and whole_lead else None
        ln = ins[0].shape[-2]
        tm = _pick(ln, tm_cands)
        grid = (ins[0].shape[0], ln // tm) if lead else (ln // tm,)
        return lead, g, tm, grid

    def fwd_call(ins, params):
        lead, g, tm, grid = geometry(ins)
        n_in, n_p = len(ins), len(params)

        def body(*refs):
            rowid = pl.program_id(1 if lead else 0) * tm + lax.broadcasted_iota(jnp.int32, (tm, 1), 0)
            outs = f(rowid, *[r[...].astype(f32) for r in refs[:n_in + n_p]])
            for o_ref, o in zip(refs[n_in + n_p:], outs):
                o_ref[...] = o.astype(o_ref.dtype)

        return pl.pallas_call(
            body, grid=grid, in_specs=specs(ins, tm, lead, g) + pspecs(params, lead),
            out_specs=specs(out_cols, tm, lead, g),
            out_shape=[SDS(ins[0].shape[:-1] + (c,), dt) for c, dt in zip(out_cols, out_dtypes)],
            name=name + "_fwd", compiler_params=_cparams())(*ins, *params)

    def bwd_call(ins, params, cts):
        lead, g, tm, grid = geometry(ins)
        n_in, n_p, n_o = len(ins), len(params), len(cts)

        def body(*refs):
            in_refs = refs[:n_in + n_p]
            ct_refs = refs[n_in + n_p:n_in + n_p + n_o]
            din_refs = refs[n_in + n_p + n_o:n_in + n_p + n_o + n_in]
            dp_refs = refs[n_in + n_p + n_o + n_in:]
            rowid = pl.program_id(1 if lead else 0) * tm + lax.broadcasted_iota(jnp.int32, (tm, 1), 0)
            _, vjp = jax.vjp(lambda *a: tuple(f(rowid, *a)), *[r[...].astype(f32) for r in in_refs])
            grads = vjp(tuple(r[...].astype(f32) for r in ct_refs))
            for r, g in zip(din_refs, grads[:n_in]):
                r[...] = g.astype(r.dtype)
            if n_p:
                first = (pl.program_id(0) == 0) & (pl.program_id(1) == 0) if lead else pl.program_id(0) == 0

                @pl.when(first)
                def _():
                    for r in dp_refs:
                        r[...] = jnp.zeros_like(r)

                for r, g in zip(dp_refs, grads[n_in:]):
                    r[...] += g

        return pl.pallas_call(
            body, grid=grid,
            in_specs=specs(ins, tm, lead, g) + pspecs(params, lead) + specs(cts, tm, lead, g),
            out_specs=specs(ins, tm, lead, g) + pspecs(params, lead),
            out_shape=[SDS(a.shape, a.dtype) for a in ins] + [SDS(p.shape, f32) for p in params],
            name=name + "_bwd", compiler_params=_cparams())(*ins, *params, *cts)

    @jax.custom_vjp
    def op(ins, params):
        return tuple(fwd_call(ins, params))

    def op_fwd(ins, params):
        return tuple(fwd_call(ins, params)), (ins, params)

    def op_bwd(res, cts):
        ins, params = res
        g = bwd_call(ins, params, cts)
        return tuple(g[:len(ins)]), tuple(g[len(ins):])

    op.defvjp(op_fwd, op_bwd)
    return op


def _rowmask(rowid):
    return (rowid >= PAD).astype(f32)


def _rms(x, g):
    return x * lax.rsqrt(jnp.mean(x * x, axis=-1, keepdims=True) + EPS) * g


def _group_mean_sq(x):
    w = x.shape[-1]
    r = lax.broadcasted_iota(jnp.int32, (w, w), 0) // HEAD_DIM
    c = lax.broadcasted_iota(jnp.int32, (w, w), 1) // HEAD_DIM
    return xdot(x * x, jnp.where(r == c, 1.0 / HEAD_DIM, 0.0).astype(f32))


def _f_rmsnorm(rowid, h, g):
    return _rms(h, g) * _rowmask(rowid), h


def _f_swiglu(rowid, gu):
    half = gu.shape[-1] // 2
    return (jax.nn.silu(gu[:, :half]) * gu[:, half:],)


def _f_headnorm(rowid, o, g):
    return (o * lax.rsqrt(_group_mean_sq(o) + EPS) * g,)


def _f_dn_out(rowid, o, z, g):
    return (o * lax.rsqrt(_group_mean_sq(o) + EPS) * g * jax.nn.silu(z),)


def _f_dn_prep(rowid, conv, ba, alog, dtb):
    tm = conv.shape[0]
    mask = _rowmask(rowid)
    s = jax.nn.silu(conv)
    q, k, v = s[:, :GW], s[:, GW:2 * GW], s[:, 2 * GW:]
    q = q * lax.rsqrt(_group_mean_sq(q) * HEAD_DIM + EPS)
    k = k * lax.rsqrt(_group_mean_sq(k) * HEAD_DIM + EPS)
    beta = jax.nn.sigmoid(ba) * mask
    g = -jnp.exp(alog) * jax.nn.softplus(ba + dtb) * mask
    r = lax.broadcasted_iota(jnp.int32, (tm, tm), 0)
    c = lax.broadcasted_iota(jnp.int32, (tm, tm), 1)
    ltri = jnp.where((r >= c) & (r // 64 == c // 64), 1.0, 0.0).astype(f32)
    gc = _select_rows(ltri, g)
    er = lax.broadcasted_iota(jnp.int32, (128, GW), 0)
    ec = lax.broadcasted_iota(jnp.int32, (128, GW), 1) // HEAD_DIM
    e_b = jnp.where(er == ec, 1.0, 0.0).astype(f32)
    e_g = jnp.where(er == ec + N_HEADS, 1.0, 0.0).astype(f32)
    return q * mask, k * mask, v * mask, xdot(gc, e_g), xdot(beta, e_b)


def _f_dn_intra(rowid, q, k, v, gc, bb):
    tm = q.shape[0]
    r = lax.broadcasted_iota(jnp.int32, (tm, tm), 0)
    c = lax.broadcasted_iota(jnp.int32, (tm, tm), 1)
    same = r // 64 == c // 64
    incl = same & (r >= c)
    strict = same & (r > c)
    eye = (r == c).astype(f32)
    gcb = jnp.broadcast_to(gc[:, 0:1], (tm, tm))
    gcr = jnp.sum(gcb * eye, axis=0, keepdims=True)
    decay = jnp.where(incl, jnp.exp(jnp.where(incl, gcb - gcr, 0.0)), 0.0)
    qs = q * (HEAD_DIM ** -0.5)
    kb = k * bb
    lmat = jnp.where(strict, bdot_nt(kb, k) * decay, 0.0)
    t = eye - lmat
    p = lmat
    for _ in range(5):
        p = bdot(p, p)
        t = t + bdot(t, p)
    egc = jnp.exp(gc)
    u = bdot(t, v * bb)
    w = bdot(t, kb * egc)
    attn_big = jnp.where(incl, bdot_nt(qs, k) * decay, 0.0)
    attn = attn_big[:, 0:64]
    for b in range(1, tm // 64):
        attn = attn + attn_big[:, 64 * b:64 * b + 64]
    sel = jnp.where(c == (r // 64) * 64 + 63, 1.0, 0.0).astype(f32)
    gl = _select_rows(sel, gc)
    return u, w, qs * egc, k * jnp.exp(gl - gc), attn, jnp.exp(gl)


def _f_dn_intra_heads(rowid, *xs):
    per_head = [_f_dn_intra(rowid, *[x[h] for x in xs]) for h in range(xs[0].shape[0])]
    return tuple(jnp.concatenate([o[j][None] for o in per_head], axis=0) for j in range(len(per_head[0])))


def _f_s5_param(rowid, ar, ai, ldt, bre, bim):
    dt = jnp.exp(ldt)
    mag = jnp.exp(ar * dt)
    abr, abi = mag * jnp.cos(ai * dt), mag * jnp.sin(ai * dt)
    den = ar * ar + ai * ai
    nr = abr - 1.0
    qr = (nr * ar + abi * ai) / den
    qi = (abi * ar - nr * ai) / den
    return abr, abi, qr * bre - qi * bim, qr * bim + qi * bre


def _f_s5_post(rowid, y, u, d, wglu, bglu, gnorm):
    y = jax.nn.gelu(y + d * u)
    o = y * jax.nn.sigmoid(bdot(y, wglu) + bglu)
    return (_rms(o, gnorm),)


def _shift_down(x, halo, r, row8):
    if r == 0:
        return x
    rolled = pltpu.roll(x, r, 0)
    top = jnp.where(row8 < r, pltpu.roll(halo, r, 0), rolled[:8])
    return jnp.concatenate([top, rolled[8:]], axis=0)


def _shift_up(x, halo, r, row8):
    if r == 0:
        return x
    tm = x.shape[0]
    rolled = pltpu.roll(x, tm - r, 0)
    bot = jnp.where(row8 >= 8 - r, pltpu.roll(halo, 8 - r, 0), rolled[tm - 8:])
    return jnp.concatenate([rolled[:tm - 8], bot], axis=0)


def _conv_call(x, w8, mode, name, dc=None):
    ln, ch = x.shape
    tm = _pick(ln, (640, 256, 128))
    n = ln // tm
    t8 = tm // 8

    def body(*refs):
        i = pl.program_id(0)
        row8 = lax.broadcasted_iota(jnp.int32, (8, ch), 0)
        if mode == "fwd":
            x_ref, h_ref, w_ref, o_ref = refs
            halo = jnp.where(i > 0, h_ref[...], 0.0)
            xv = x_ref[...]
            acc = jnp.zeros((tm, ch), f32)
            for j in range(DN_CONV):
                acc = acc + w_ref[j:j + 1, :] * _shift_down(xv, halo, DN_CONV - 1 - j, row8)
            o_ref[...] = acc
        elif mode == "dx":
            x_ref, h_ref, w_ref, o_ref = refs
            halo = jnp.where(i < n - 1, h_ref[...], 0.0)
            xv = x_ref[...]
            acc = jnp.zeros((tm, ch), f32)
            for j in range(DN_CONV):
                acc = acc + w_ref[j:j + 1, :] * _shift_up(xv, halo, DN_CONV - 1 - j, row8)
            o_ref[...] = acc
        else:
            x_ref, h_ref, dc_ref, o_ref = refs
            halo = jnp.where(i > 0, h_ref[...], 0.0)
            xv, dcv = x_ref[...], dc_ref[...]
            acc = jnp.zeros((8, ch), f32)
            for j in range(DN_CONV):
                s = jnp.sum(_shift_down(xv, halo, DN_CONV - 1 - j, row8) * dcv, axis=0, keepdims=True)
                acc = acc + jnp.where(row8 == j, s, 0.0)

            @pl.when(i == 0)
            def _():
                o_ref[...] = jnp.zeros_like(o_ref)

            o_ref[...] += acc

    blk = pl.BlockSpec((tm, ch), lambda i: (i, 0))
    if mode == "dx":
        halo_spec = pl.BlockSpec((8, ch), lambda i: (jnp.minimum((i + 1) * t8, n * t8 - 1), 0))
    else:
        halo_spec = pl.BlockSpec((8, ch), lambda i: (jnp.maximum(i * t8 - 1, 0), 0))
    small = pl.BlockSpec((8, ch), lambda i: (0, 0))
    if mode == "dw":
        return pl.pallas_call(body, grid=(n,), in_specs=[blk, halo_spec, blk], out_specs=small,
                              out_shape=SDS((8, ch), f32), name=name, compiler_params=_cparams())(x, x, dc)
    return pl.pallas_call(body, grid=(n,), in_specs=[blk, halo_spec, small], out_specs=blk,
                          out_shape=SDS((ln, ch), f32), name=name, compiler_params=_cparams())(x, x, w8)


@jax.custom_vjp
def conv_op(x, w8):
    return _conv_call(x, w8, "fwd", "dn_conv_fwd")


def _conv_fwd(x, w8):
    return _conv_call(x, w8, "fwd", "dn_conv_fwd"), (x, w8)


def _conv_bwd(res, dc):
    x, w8 = res
    return _conv_call(dc, w8, "dx", "dn_conv_dx"), _conv_call(x, None, "dw", "dn_conv_dw", dc=dc)


conv_op.defvjp(_conv_fwd, _conv_bwd)


SB_DEAD = -104.0
SB_UNSEEN = -1e30


def _softplus(z):
    return jnp.maximum(z, 0.0) + jnp.log1p(jnp.exp(-jnp.abs(z)))


def _scan_matrix(suffix):
    r = lax.broadcasted_iota(jnp.int32, (256, 256), 0) % 128
    c = lax.broadcasted_iota(jnp.int32, (256, 256), 1)
    inside = (r > c) if suffix else (r < c)
    return jnp.where((c >= 128) | inside, 1.0, 0.0).astype(bf16)


def _block_sums(x, mat):
    hi = x.astype(bf16)
    lo = (x - hi.astype(f32)).astype(bf16)
    r = jnp.dot(jnp.concatenate([hi, lo], axis=1), mat, preferred_element_type=f32)
    return r[:, :128], r[:, 128:]


def _sb_scores(q, k, k0, qpos, suf_mat, c_lk):
    z = lax.dot_general(q, k, _DIMS["nt"], preferred_element_type=f32)
    kpos = k0 + lax.broadcasted_iota(jnp.int32, z.shape, 1)
    valid = (kpos < qpos) & (kpos >= PAD)
    sp = _softplus(z)
    lk = jnp.where(valid, -sp, 0.0)
    suf, tot = _block_sums(lk, suf_mat)
    w = jnp.where(valid, jnp.exp(z - sp + suf + c_lk), 0.0)
    return z, sp, valid, w, tot


def _sb_fwd_call(qs, kb, vb):
    nh, ln, hd = qs.shape
    tq = _pick(ln, (256, 128))
    nsub = tq // 128
    assert nsub == 2
    assert ln // 128 <= 256

    def body(q_ref, k_ref, v_ref, o_ref, c_ref):
        qi = pl.program_id(1)
        q = q_ref[...]
        suf_mat = _scan_matrix(True)
        qpos = qi * tq + lax.broadcasted_iota(jnp.int32, (tq, 128), 0)
        lane = lax.broadcasted_iota(jnp.int32, (tq, 256), 1)

        def live(carry):
            return (carry[0] >= 0) & (carry[1] > 0)

        def step(carry):
            t, _, c_lk, acc, saved = carry
            for tt in (t, t - 1):
                k0 = pl.multiple_of(tt * 128, 128)
                k = k_ref[pl.ds(k0, 128), :]
                v = v_ref[pl.ds(k0, 128), :]
                saved = jnp.where(lane == tt, jnp.concatenate([c_lk, c_lk], axis=1), saved)
                _, _, _, w, tot = _sb_scores(q, k, k0, qpos, suf_mat, c_lk)
                acc = acc + jnp.dot(w.astype(bf16), v, preferred_element_type=f32)
                c_lk = c_lk + tot
            return t - 2, (jnp.max(c_lk) >= SB_DEAD).astype(jnp.int32), c_lk, acc, saved

        init = ((qi + 1) * nsub - 1, jnp.int32(1), jnp.zeros((tq, 128), f32), jnp.zeros((tq, hd), f32),
                jnp.full((tq, 256), SB_UNSEEN, f32))
        _, _, _, acc, saved = lax.while_loop(live, step, init)
        o_ref[...] = acc
        c_ref[...] = saved

    full = pl.BlockSpec((None, ln, hd), lambda h, i: (h, 0, 0))
    blk = pl.BlockSpec((None, tq, hd), lambda h, i: (h, i, 0))
    cblk = pl.BlockSpec((None, tq, 256), lambda h, i: (h, i, 0))
    return pl.pallas_call(body, grid=(nh, ln // tq), in_specs=[blk, full, full], out_specs=[blk, cblk],
                          out_shape=[SDS((nh, ln, hd), f32), SDS((nh, ln, 256), f32)], name="sb_attn_fwd",
                          compiler_params=_cparams())(qs, kb, vb)


def _sb_bwd_call(qs, kb, vb, carries, do):
    nh, ln, hd = qs.shape
    tq = _pick(ln, (256, 128))
    nsub = tq // 128
    assert nsub == 2
    nq = ln // tq

    def body(q_ref, k_ref, v_ref, c_ref, do_ref, dq_ref, dk_hbm, dv_hbm, dk_ref, dv_ref):
        qi = pl.program_id(1)

        @pl.when(qi == 0)
        def _():
            dk_ref[...] = jnp.zeros_like(dk_ref)
            dv_ref[...] = jnp.zeros_like(dv_ref)

        q = q_ref[...]
        dob = do_ref[...].astype(bf16)
        saved = c_ref[...]
        suf_mat = _scan_matrix(True)
        pre_mat = _scan_matrix(False)
        qpos = qi * tq + lax.broadcasted_iota(jnp.int32, (tq, 128), 0)
        lane = lax.broadcasted_iota(jnp.int32, (tq, 256), 1)

        def step(i, carry):
            c_e, dq = carry
            for t in (2 * i, 2 * i + 1):
                k0 = pl.multiple_of(t * 128, 128)
                k = k_ref[pl.ds(k0, 128), :]
                v = v_ref[pl.ds(k0, 128), :]
                c_lk = jnp.sum(jnp.where(lane == t, saved, 0.0), axis=1, keepdims=True)
                z, sp, valid, w, _ = _sb_scores(q, k, k0, qpos, suf_mat, c_lk)
                e = lax.dot_general(dob, v, _DIMS["nt"], preferred_element_type=f32) * w
                pre_e, tot_e = _block_sums(e, pre_mat)
                sig = jnp.exp(z - sp)
                dz = jnp.where(valid, e * (1.0 - sig) - sig * (pre_e + c_e), 0.0).astype(bf16)
                dq = dq + jnp.dot(dz, k, preferred_element_type=f32)
                dk_ref[pl.ds(k0, 128), :] += lax.dot_general(dz, q, _DIMS["tn"], preferred_element_type=f32)
                dv_ref[pl.ds(k0, 128), :] += lax.dot_general(w.astype(bf16), dob, _DIMS["tn"], preferred_element_type=f32)
                c_e = c_e + tot_e
            return c_e, dq

        ntile = (qi + 1) * nsub
        lane1 = lax.broadcasted_iota(jnp.int32, (1, 256), 1)
        dead = (jnp.max(saved, axis=0, keepdims=True) < SB_DEAD) & (lane1 < ntile)
        first = jnp.sum(dead.astype(jnp.int32))
        _, dq = lax.fori_loop(first // 2, ntile // 2, step, (jnp.zeros((tq, 128), f32), jnp.zeros((tq, hd), f32)))
        dq_ref[...] = dq * (HEAD_DIM ** -0.5)

        @pl.when(qi == nq - 1)
        def _():
            pltpu.sync_copy(dk_ref, dk_hbm.at[pl.program_id(0)])
            pltpu.sync_copy(dv_ref, dv_hbm.at[pl.program_id(0)])

    full = pl.BlockSpec((None, ln, hd), lambda h, i: (h, 0, 0))
    blk = pl.BlockSpec((None, tq, hd), lambda h, i: (h, i, 0))
    cblk = pl.BlockSpec((None, tq, 256), lambda h, i: (h, i, 0))
    anyspec = pl.BlockSpec(memory_space=pl.ANY)
    return pl.pallas_call(body, grid=(nh, nq), in_specs=[blk, full, full, cblk, blk],
                          out_specs=[blk, anyspec, anyspec], out_shape=[SDS((nh, ln, hd), f32)] * 3,
                          scratch_shapes=[pltpu.VMEM((ln, hd), f32)] * 2,
                          name="sb_attn_bwd", compiler_params=_cparams())(qs, kb, vb, carries, do)


def _sb_operands(q, k, v):
    return (q * (HEAD_DIM ** -0.5)).astype(bf16), k.astype(bf16), v.astype(bf16)


@jax.custom_vjp
def sb_attention(q, k, v):
    return _sb_fwd_call(*_sb_operands(q, k, v))[0]


def _sb_fwd(q, k, v):
    qs, kb, vb = _sb_operands(q, k, v)
    o, carries = _sb_fwd_call(qs, kb, vb)
    return o, (qs, kb, vb, carries)


def _sb_bwd(res, do):
    return tuple(_sb_bwd_call(*res, do))


sb_attention.defvjp(_sb_fwd, _sb_bwd)


def _dn_scan_geometry(ln):
    tb = _pick(ln, (640, 256))
    return tb, ln // tb, tb // 64


def _dn_scan_fwd_call(u, w, qg, kg, attn, egl):
    nh, ln, hd = u.shape
    tb, nblk, nck = _dn_scan_geometry(ln)

    def body(u_ref, w_ref, qg_ref, kg_ref, a_ref, e_ref, o_ref, hist_ref, s_ref):
        @pl.when(pl.program_id(0) == 0)
        def _():
            s_ref[...] = jnp.zeros_like(s_ref)

        def chunk(ci, _):
            rows = pl.ds(pl.multiple_of(ci * 64, 64), 64)
            for h in range(nh):
                s = s_ref[h]
                hist_ref[ci, h] = s
                v_new = u_ref[h, rows, :] - _dot16(w_ref[h, rows, :], s)
                o_ref[h, rows, :] = _dot16(qg_ref[h, rows, :], s) + _dot16(a_ref[h, rows, :], v_new)
                s_ref[h] = s * e_ref[h, rows, :][0:1, :] + _dot16(kg_ref[h, rows, :], v_new, "tn")
            return 0

        lax.fori_loop(0, nck, chunk, 0)

    blk = pl.BlockSpec((nh, tb, hd), lambda i: (0, i, 0))
    return pl.pallas_call(
        body, grid=(nblk,), in_specs=[blk] * 6,
        out_specs=[blk, pl.BlockSpec((nck, nh, hd, hd), lambda i: (i, 0, 0, 0))],
        out_shape=[SDS((nh, ln, hd), f32), SDS((ln // 64, nh, hd, hd), f32)],
        scratch_shapes=[pltpu.VMEM((nh, hd, hd), f32)], name="dn_scan_fwd", compiler_params=_cparams())(u, w, qg, kg, attn, egl)


def _dn_scan_bwd_call(u, w, qg, kg, attn, egl, hist, do):
    nh, ln, hd = u.shape
    tb, nblk, nck = _dn_scan_geometry(ln)

    def body(u_ref, w_ref, qg_ref, kg_ref, a_ref, e_ref, hist_ref, do_ref,
             du_ref, dw_ref, dqg_ref, dkg_ref, da_ref, de_ref, ds_ref):
        @pl.when(pl.program_id(0) == 0)
        def _():
            ds_ref[...] = jnp.zeros_like(ds_ref)

        row0 = lax.broadcasted_iota(jnp.int32, (64, hd), 0) == 0

        def chunk(i, _):
            ci = nck - 1 - i
            rows = pl.ds(pl.multiple_of(ci * 64, 64), 64)
            for h in range(nh):
                s = hist_ref[ci, h]
                ds = ds_ref[h]
                dov = do_ref[h, rows, :]
                wv, kgv, av = w_ref[h, rows, :], kg_ref[h, rows, :], a_ref[h, rows, :]
                egl_row = e_ref[h, rows, :][0:1, :]
                v_new = u_ref[h, rows, :] - _dot16(wv, s)
                dv_new = _dot16(av, dov, "tn") + _dot16(kgv, ds)
                du_ref[h, rows, :] = dv_new
                dw_ref[h, rows, :] = -_dot16(dv_new, s, "nt")
                dqg_ref[h, rows, :] = _dot16(dov, s, "nt")
                dkg_ref[h, rows, :] = _dot16(v_new, ds, "nt")
                da_ref[h, rows, :] = _dot16(dov, v_new, "nt")
                de_ref[h, rows, :] = jnp.where(row0, jnp.sum(s * ds, axis=0, keepdims=True), 0.0)
                ds_ref[h] = ds * egl_row + _dot16(qg_ref[h, rows, :], dov, "tn") - _dot16(wv, dv_new, "tn")
            return 0

        lax.fori_loop(0, nck, chunk, 0)

    blk = pl.BlockSpec((nh, tb, hd), lambda i: (0, nblk - 1 - i, 0))
    hblk = pl.BlockSpec((nck, nh, hd, hd), lambda i: (nblk - 1 - i, 0, 0, 0))
    return pl.pallas_call(
        body, grid=(nblk,), in_specs=[blk] * 6 + [hblk, blk], out_specs=[blk] * 6,
        out_shape=[SDS((nh, ln, hd), f32)] * 6, scratch_shapes=[pltpu.VMEM((nh, hd, hd), f32)],
        name="dn_scan_bwd", compiler_params=_cparams())(u, w, qg, kg, attn, egl, hist, do)


@jax.custom_vjp
def dn_scan(u, w, qg, kg, attn, egl):
    return _dn_scan_fwd_call(u, w, qg, kg, attn, egl)[0]


def _dn_scan_fwd(u, w, qg, kg, attn, egl):
    o, hist = _dn_scan_fwd_call(u, w, qg, kg, attn, egl)
    return o, (u, w, qg, kg, attn, egl, hist)


def _dn_scan_bwd(res, do):
    return tuple(_dn_scan_bwd_call(*res, do))


dn_scan.defvjp(_dn_scan_fwd, _dn_scan_bwd)


def _cmul(a, b):
    return a[0] * b[0] - a[1] * b[1], a[0] * b[1] + a[1] * b[0]


def _powers(a1):
    a2 = _cmul(a1, a1)
    a3 = _cmul(a2, a1)
    a4 = _cmul(a2, a2)
    return [a1, a2, a3, a4, _cmul(a4, a1), _cmul(a4, a2), _cmul(a4, a3), _cmul(a4, a4)]


def _row_table(pw, row, order):
    tr = jnp.zeros(row.shape, f32)
    ti = jnp.zeros(row.shape, f32)
    for r in range(8):
        p = pw[order(r)]
        tr = tr + jnp.where(row == r, p[0], 0.0)
        ti = ti + jnp.where(row == r, p[1], 0.0)
    return tr, ti


def _s5_fwd_call(u, bmat, cmat, are, aim):
    ln, wu = u.shape
    w2 = bmat.shape[1]
    nl = w2 // 2
    tm = _pick(ln, (256, 128))

    def body(u_ref, b_ref, c_ref, are_ref, aim_ref, st_ref, y_ref, bu_ref, cr_ref, ci_ref):
        @pl.when(pl.program_id(0) == 0)
        def _():
            cr_ref[...] = jnp.zeros_like(cr_ref)
            ci_ref[...] = jnp.zeros_like(ci_ref)

        uv = u_ref[...].astype(bf16)
        for j in range(nl // 256):
            uj = uv[:, 64 * j:64 * j + 64]
            for half in (0, nl):
                cols = slice(half + 256 * j, half + 256 * (j + 1))
                bu_ref[:, cols] = jnp.dot(uj, b_ref[64 * j:64 * j + 64, cols], preferred_element_type=f32)
        pw = _powers((are_ref[...], aim_ref[...]))
        row = lax.broadcasted_iota(jnp.int32, (8, nl), 0)
        table = _row_table(pw, row, lambda r: r)

        def tile(t, c):
            rows = pl.ds(pl.multiple_of(t * 8, 8), 8)
            x = (bu_ref[rows, 0:nl], bu_ref[rows, nl:w2])
            for kk in (1, 2, 4):
                sh = (jnp.where(row >= kk, pltpu.roll(x[0], kk, 0), 0.0), jnp.where(row >= kk, pltpu.roll(x[1], kk, 0), 0.0))
                m = _cmul(pw[kk - 1], sh)
                x = (x[0] + m[0], x[1] + m[1])
            m = _cmul(table, c)
            x = (x[0] + m[0], x[1] + m[1])
            st_ref[rows, 0:nl] = x[0]
            st_ref[rows, nl:w2] = x[1]
            return (jnp.sum(jnp.where(row == 7, x[0], 0.0), axis=0, keepdims=True),
                    jnp.sum(jnp.where(row == 7, x[1], 0.0), axis=0, keepdims=True))

        c = lax.fori_loop(0, tm // 8, tile, (cr_ref[...], ci_ref[...]), unroll=4)
        cr_ref[...] = c[0]
        ci_ref[...] = c[1]
        for t in range(wu // 256):
            ycols = slice(256 * t, 256 * (t + 1))
            parts = [jnp.dot(st_ref[:, half + 1024 * t:half + 1024 * (t + 1)].astype(bf16),
                             c_ref[half + 1024 * t:half + 1024 * (t + 1), ycols], preferred_element_type=f32) for half in (0, nl)]
            y_ref[:, ycols] = parts[0] + parts[1]

    vec = pl.BlockSpec((1, nl), lambda i: (0, 0))
    whole = lambda a: pl.BlockSpec(a.shape, lambda i: (0, 0))
    rows = lambda c: pl.BlockSpec((tm, c), lambda i: (i, 0))
    return pl.pallas_call(
        body, grid=(ln // tm,), in_specs=[rows(wu), whole(bmat), whole(cmat), vec, vec], out_specs=[rows(w2), rows(wu)],
        out_shape=[SDS((ln, w2), f32), SDS((ln, wu), f32)],
        scratch_shapes=[pltpu.VMEM((tm, w2), f32), pltpu.VMEM((1, nl), f32), pltpu.VMEM((1, nl), f32)],
        name="s5_fwd", compiler_params=_cparams())(u, bmat, cmat, are, aim)


def _s5_bwd_call(u, bmat, cmat, are, aim, st, dy):
    ln, wu = u.shape
    w2 = bmat.shape[1]
    nl = w2 // 2
    tm = _pick(ln, (256, 128))
    n = ln // tm
    t8 = tm // 8

    def body(u_ref, st_ref, halo_ref, dy_ref, b_hbm, c_hbm, are_ref, aim_ref, du_ref, db_hbm, dc_hbm, dar_ref, dai_ref,
             b_ref, c_ref, db_ref, dc_ref, sb_ref, d_ref, cr_ref, ci_ref):
        i = pl.program_id(0)
        g_ref = d_ref

        @pl.when(i == 0)
        def _():
            pltpu.sync_copy(b_hbm, b_ref)
            pltpu.sync_copy(c_hbm, c_ref)
            db_ref[...] = jnp.zeros_like(db_ref)
            dc_ref[...] = jnp.zeros_like(dc_ref)
            cr_ref[...] = jnp.zeros_like(cr_ref)
            ci_ref[...] = jnp.zeros_like(ci_ref)
            dar_ref[...] = jnp.zeros_like(dar_ref)
            dai_ref[...] = jnp.zeros_like(dai_ref)

        dyv = dy_ref[...].astype(bf16)
        for t in range(wu // 256):
            for half in (0, nl):
                lanes = slice(half + 1024 * t, half + 1024 * (t + 1))
                d_ref[:, lanes] = lax.dot_general(dyv[:, 256 * t:256 * (t + 1)], c_ref[lanes, 256 * t:256 * (t + 1)],
                                                  _DIMS["nt"], preferred_element_type=f32)
        sb_ref[0:8, :] = jnp.where(i < n - 1, halo_ref[...], 0.0)
        sb_ref[8:tm + 8, :] = st_ref[...]
        pw = _powers((are_ref[...], -aim_ref[...]))
        row = lax.broadcasted_iota(jnp.int32, (8, nl), 0)
        table = _row_table(pw, row, lambda r: 7 - r)

        def tile(j, carry):
            c, acc_r, acc_i = carry
            t = t8 - 1 - j
            rows = pl.ds(pl.multiple_of(t * 8, 8), 8)
            x = (d_ref[rows, 0:nl], d_ref[rows, nl:w2])
            for kk in (1, 2, 4):
                sh = (jnp.where(row < 8 - kk, pltpu.roll(x[0], 8 - kk, 0), 0.0),
                      jnp.where(row < 8 - kk, pltpu.roll(x[1], 8 - kk, 0), 0.0))
                m = _cmul(pw[kk - 1], sh)
                x = (x[0] + m[0], x[1] + m[1])
            m = _cmul(table, c)
            x = (x[0] + m[0], x[1] + m[1])
            g_ref[rows, 0:nl] = x[0]
            g_ref[rows, nl:w2] = x[1]
            prev = (sb_ref[rows, 0:nl], sb_ref[rows, nl:w2])
            cur = (st_ref[rows, 0:nl], st_ref[rows, nl:w2])
            last_r = jnp.sum(jnp.where(row == 7, prev[0], 0.0), axis=0, keepdims=True)
            last_i = jnp.sum(jnp.where(row == 7, prev[1], 0.0), axis=0, keepdims=True)
            sp = (jnp.where(row == 0, last_r, pltpu.roll(cur[0], 1, 0)), jnp.where(row == 0, last_i, pltpu.roll(cur[1], 1, 0)))
            acc_r = acc_r + x[0] * sp[0] + x[1] * sp[1]
            acc_i = acc_i + x[1] * sp[0] - x[0] * sp[1]
            c = (jnp.sum(jnp.where(row == 0, x[0], 0.0), axis=0, keepdims=True),
                 jnp.sum(jnp.where(row == 0, x[1], 0.0), axis=0, keepdims=True))
            return c, acc_r, acc_i

        z8 = jnp.zeros((8, nl), f32)
        c, acc_r, acc_i = lax.fori_loop(0, t8, tile, ((cr_ref[...], ci_ref[...]), z8, z8), unroll=4)
        cr_ref[...] = c[0]
        ci_ref[...] = c[1]
        dar_ref[...] += jnp.sum(acc_r, axis=0, keepdims=True)
        dai_ref[...] += jnp.sum(acc_i, axis=0, keepdims=True)
        g = g_ref[...].astype(bf16)
        du_ref[...] = lax.dot_general(g, b_ref[...], _DIMS["nt"], preferred_element_type=f32)
        uv = u_ref[...].astype(bf16)
        for j in range(nl // 256):
            for half in (0, nl):
                cols = slice(half + 256 * j, half + 256 * (j + 1))
                db_ref[64 * j:64 * j + 64, cols] += lax.dot_general(uv[:, 64 * j:64 * j + 64], g[:, cols], _DIMS["tn"],
                                                                     preferred_element_type=f32)
        for t in range(wu // 256):
            for half in (0, nl):
                lanes = slice(half + 1024 * t, half + 1024 * (t + 1))
                dc_ref[lanes, 256 * t:256 * (t + 1)] += lax.dot_general(
                    st_ref[:, lanes].astype(bf16), dyv[:, 256 * t:256 * (t + 1)], _DIMS["tn"], preferred_element_type=f32)

        @pl.when(i == n - 1)
        def _():
            pltpu.sync_copy(db_ref, db_hbm)
            pltpu.sync_copy(dc_ref, dc_hbm)

    vec = pl.BlockSpec((1, nl), lambda i: (0, 0))
    rows = lambda c: pl.BlockSpec((tm, c), lambda i: (n - 1 - i, 0))
    halo = pl.BlockSpec((8, w2), lambda i: (jnp.maximum((n - 1 - i) * t8 - 1, 0), 0))
    anyspec = pl.BlockSpec(memory_space=pl.ANY)
    return pl.pallas_call(
        body, grid=(n,), in_specs=[rows(wu), rows(w2), halo, rows(wu), anyspec, anyspec, vec, vec],
        out_specs=[rows(wu), anyspec, anyspec, vec, vec],
        out_shape=[SDS((ln, wu), f32), SDS(bmat.shape, f32), SDS(cmat.shape, f32), SDS((1, nl), f32), SDS((1, nl), f32)],
        scratch_shapes=[pltpu.VMEM(bmat.shape, bf16), pltpu.VMEM(cmat.shape, bf16), pltpu.VMEM(bmat.shape, f32),
                        pltpu.VMEM(cmat.shape, f32), pltpu.VMEM((tm + 8, w2), f32), pltpu.VMEM((tm, w2), f32),
                        pltpu.VMEM((1, nl), f32), pltpu.VMEM((1, nl), f32)],
        name="s5_bwd", compiler_params=_cparams())(u, st, st, dy, bmat, cmat, are, aim)


@jax.custom_vjp
def s5_core(u, bmat, cmat, are, aim):
    return _s5_fwd_call(u, bmat.astype(bf16), cmat.astype(bf16), are, aim)[1]


def _s5_core_fwd(u, bmat, cmat, are, aim):
    bb, cb = bmat.astype(bf16), cmat.astype(bf16)
    st, y = _s5_fwd_call(u, bb, cb, are, aim)
    return y, (u, bb, cb, are, aim, st)


def _s5_core_bwd(res, dy):
    return tuple(_s5_bwd_call(*res, dy))


s5_core.defvjp(_s5_core_fwd, _s5_core_bwd)


def _final_call(h, target, gnorm):
    ln, d = h.shape
    tm = PAD + N_META
    skip = (ln - target.shape[0]) // tm

    def body(h_ref, t_ref, g_ref, loss_ref, dh_ref, dg_ref):
        i = pl.program_id(0)

        @pl.when(i == 0)
        def _():
            loss_ref[...] = jnp.zeros_like(loss_ref)
            dg_ref[...] = jnp.zeros_like(dg_ref)

        live = (i >= skip).astype(f32)
        y, vjp = jax.vjp(_rms, h_ref[...], g_ref[...])
        err = (y - t_ref[...]) * live
        loss_ref[...] += 0.5 * jnp.sum(jnp.sum(err * err, axis=1, keepdims=True) / d, axis=0, keepdims=True)
        dh, dg = vjp(err / d)
        dh_ref[...] = dh
        dg_ref[...] += dg

    return pl.pallas_call(
        body, grid=(ln // tm,),
        in_specs=[pl.BlockSpec((tm, d), lambda i: (i, 0)), pl.BlockSpec((tm, d), lambda i: (jnp.maximum(i - skip, 0), 0)),
                  pl.BlockSpec((1, d), lambda i: (0, 0))],
        out_specs=[pl.BlockSpec((1, 1), lambda i: (0, 0)), pl.BlockSpec((tm, d), lambda i: (i, 0)), pl.BlockSpec((1, d), lambda i: (0, 0))],
        out_shape=[SDS((1, 1), f32), SDS((ln, d), f32), SDS((1, d), f32)], name="final_loss", compiler_params=_cparams())(h, target, gnorm)


def _exchange(x, gather, name):
    r, c = x.shape[-2:]

    def body(x_ref, o_ref, send_sems, recv_sems, local_sem):
        ax, ay, ac = lax.axis_index("x"), lax.axis_index("y"), lax.axis_index("c")
        me = 4 * ax + 2 * ay + ac
        local = pltpu.make_async_copy(x_ref if gather else x_ref.at[me], o_ref.at[me], local_sem)
        local.start()
        sent = []
        for k in range(1, N_DEV):
            px = 1 - ax if k & 4 else ax
            py = 1 - ay if k & 2 else ay
            pc = 1 - ac if k & 1 else ac
            p = 4 * px + 2 * py + pc
            cp = pltpu.make_async_remote_copy(
                src_ref=x_ref if gather else x_ref.at[p], dst_ref=o_ref.at[me], send_sem=send_sems.at[k - 1],
                recv_sem=recv_sems.at[k - 1], device_id=(px, py, pc), device_id_type=pl.DeviceIdType.MESH)
            cp.start()
            sent.append((cp, p, (px, py, pc)))
        for k, (cp, p, peer) in enumerate(sent):
            pltpu.make_async_remote_copy(
                src_ref=o_ref.at[p], dst_ref=o_ref.at[p], send_sem=send_sems.at[k], recv_sem=recv_sems.at[k],
                device_id=peer, device_id_type=pl.DeviceIdType.MESH).wait_recv()
        for cp, _, _ in sent:
            cp.wait_send()
        local.wait()

    hbm = pl.BlockSpec(memory_space=pltpu.HBM)
    return pl.pallas_call(
        body, in_specs=[hbm], out_specs=hbm, out_shape=SDS((N_DEV, r, c), x.dtype),
        scratch_shapes=[pltpu.SemaphoreType.DMA((N_DEV - 1,)), pltpu.SemaphoreType.DMA((N_DEV - 1,)), pltpu.SemaphoreType.DMA],
        name=name)(x)


def _gather_via_sibling(x, name):
    r, c = x.shape

    def body(x_ref, o_ref, send_sems, recv_sems, local_sem):
        ax, ay, ac = lax.axis_index("x"), lax.axis_index("y"), lax.axis_index("c")
        me, sibling = (ax, ay, ac), (ax, ay, 1 - ac)
        chips = [(1 - ax, ay), (ax, 1 - ay), (1 - ax, 1 - ay)]

        def slot(px, py, pc):
            return o_ref.at[4 * px + 2 * py + pc]

        def copy(k, block, to, src=None):
            return pltpu.make_async_remote_copy(
                src_ref=slot(*block) if src is None else src, dst_ref=slot(*block), send_sem=send_sems.at[k],
                recv_sem=recv_sems.at[k], device_id=to, device_id_type=pl.DeviceIdType.MESH)

        mine = pltpu.make_async_copy(x_ref, slot(*me), local_sem)
        mine.start()
        first = [copy(0, me, sibling, src=x_ref)] + [copy(1 + j, me, (*chip, ac), src=x_ref) for j, chip in enumerate(chips)]
        for cp in first:
            cp.start()
        passed = [copy(4 + j, (*chip, ac), sibling) for j, chip in enumerate(chips)]
        for j, chip in enumerate(chips):
            copy(1 + j, (*chip, ac), me).wait_recv()
            passed[j].start()
        copy(0, sibling, me).wait_recv()
        for j, chip in enumerate(chips):
            copy(4 + j, (*chip, 1 - ac), me).wait_recv()
        for cp in first + passed:
            cp.wait_send()
        mine.wait()

    hbm = pl.BlockSpec(memory_space=pltpu.HBM)
    return pl.pallas_call(
        body, in_specs=[hbm], out_specs=hbm, out_shape=SDS((N_DEV, r, c), x.dtype),
        scratch_shapes=[pltpu.SemaphoreType.DMA((N_DEV - 1,)), pltpu.SemaphoreType.DMA((N_DEV - 1,)), pltpu.SemaphoreType.DMA],
        name=name)(x)


def _sum_slots(x, name):
    _, r, c = x.shape
    tr = _pick(r, (PACK_BLOCK_ROWS, 256, 128, 64, 32, 16, 8))

    def body(x_ref, o_ref):
        acc = x_ref[0].astype(f32)
        for p in range(1, N_DEV):
            acc = acc + x_ref[p].astype(f32)
        o_ref[...] = acc

    return pl.pallas_call(body, grid=(r // tr,), in_specs=[pl.BlockSpec((N_DEV, tr, c), lambda i: (0, i, 0))],
                          out_specs=pl.BlockSpec((tr, c), lambda i: (i, 0)), out_shape=SDS((r, c), f32), name=name,
                          compiler_params=_cparams())(x)


def _adamw(w, g, m, v, name):
    shape = w.shape
    w, g, m, v = (t.reshape((-1, shape[-1]) if t.ndim > 1 else (1, -1)) for t in (w, g, m, v))
    r, c = w.shape
    tr = _pick(r, (512, 256, 128, 64, 32, 16, 8)) if r % 8 == 0 else r

    def body(w_ref, g_ref, m_ref, v_ref, d_ref, mo_ref, vo_ref):
        gv = g_ref[...]
        mn = ADAM_B1 * m_ref[...] + (1.0 - ADAM_B1) * gv
        vn = ADAM_B2 * v_ref[...] + (1.0 - ADAM_B2) * jnp.square(gv)
        m_hat = mn / (1.0 - ADAM_B1 ** ADAM_STEP)
        v_hat = vn / (1.0 - ADAM_B2 ** ADAM_STEP)
        d_ref[...] = -ADAM_LR * (m_hat / (jnp.sqrt(v_hat) + ADAM_EPS) + ADAM_WD * w_ref[...])
        mo_ref[...] = mn
        vo_ref[...] = vn

    blk = pl.BlockSpec((tr, c), lambda i: (i, 0))
    outs = pl.pallas_call(body, grid=(r // tr,), in_specs=[blk] * 4, out_specs=[blk] * 3, out_shape=[SDS((r, c), f32)] * 3,
                          name=name, compiler_params=_cparams())(w, g, m, v)
    return [t.reshape(shape) for t in outs]


_WEIGHTS = [
    ("meta_tokens", (N_META, D_MODEL), 1), ("ffn1_norm", (DEPTH, D_MODEL), None),
    ("ffn1_w_gate", (DEPTH, D_MODEL, D_FF), 2), ("ffn1_w_up", (DEPTH, D_MODEL, D_FF), 2),
    ("ffn1_w_down", (DEPTH, D_FF, D_MODEL), 1), ("mix_norm", (DEPTH, D_MODEL), None),
    ("w_in", (DEPTH, D_MODEL, IN_WIDTH), 2), ("sb_out_norm", (DEPTH, HEAD_DIM), None),
    ("dn_conv_w", (DEPTH, DN_CONV, 3 * GW), 2), ("dn_a_log", (DEPTH, N_HEADS), None),
    ("dn_dt_bias", (DEPTH, N_HEADS), None), ("dn_out_norm", (DEPTH, HEAD_DIM), None),
    ("s5_a_re", (DEPTH, S5_GROUPS, S5_STATE), None), ("s5_a_im", (DEPTH, S5_GROUPS, S5_STATE), None),
    ("s5_log_dt", (DEPTH, S5_GROUPS), None), ("s5_b_re", (DEPTH, S5_GROUPS, S5_STATE, S5_GROUP), None),
    ("s5_b_im", (DEPTH, S5_GROUPS, S5_STATE, S5_GROUP), None), ("s5_c_re", (DEPTH, S5_GROUPS, S5_GROUP, S5_STATE), None),
    ("s5_c_im", (DEPTH, S5_GROUPS, S5_GROUP, S5_STATE), None), ("s5_d", (DEPTH, S5_WIDTH), None),
    ("s5_w_glu", (DEPTH, S5_WIDTH, S5_WIDTH), 1), ("s5_b_glu", (DEPTH, S5_WIDTH), None),
    ("s5_out_norm", (DEPTH, S5_WIDTH), None), ("w_out", (DEPTH, D_MODEL, D_MODEL), 1),
    ("ffn2_norm", (DEPTH, D_MODEL), None), ("ffn2_w_gate", (DEPTH, D_MODEL, D_FF), 2),
    ("ffn2_w_up", (DEPTH, D_MODEL, D_FF), 2), ("ffn2_w_down", (DEPTH, D_FF, D_MODEL), 1),
    ("final_norm", (D_MODEL,), None),
]
_SHARDED = [(n, s, a) for n, s, a in _WEIGHTS if a is not None]
_REPL = [(n, s) for n, s, a in _WEIGHTS if a is None]
PACK_ROW_ALIGN = 16
PACK_BLOCK_ROWS = 256


def _shard_shape(shape, axis):
    return tuple(d // N_DEV if i == axis else d for i, d in enumerate(shape))


def _pack_rows_of(n):
    rows = -(-n // PACK_COLS)
    return -(-rows // PACK_ROW_ALIGN) * PACK_ROW_ALIGN


def _as_rows(t, lead):
    head = t.shape[:lead]
    n = math.prod(t.shape[lead:])
    rows = _pack_rows_of(n)
    if n % PACK_COLS == 0:
        t = t.reshape(head + (n // PACK_COLS, PACK_COLS))
        return jnp.pad(t, [(0, 0)] * lead + [(0, rows - n // PACK_COLS), (0, 0)])
    t = jnp.pad(t.reshape(head + (n,)), [(0, 0)] * lead + [(0, rows * PACK_COLS - n)])
    return t.reshape(head + (rows, PACK_COLS))


def _pack(parts, lead=0):
    rows = [_as_rows(p, lead) for p in parts]
    total = sum(r.shape[lead] for r in rows)
    fill = -total % PACK_BLOCK_ROWS
    if fill:
        rows.append(jnp.zeros(rows[0].shape[:lead] + (fill, PACK_COLS), rows[0].dtype))
    return jnp.concatenate(rows, axis=lead)


def _unpack(pack, shapes, lead=0):
    out, off = [], 0
    head = pack.shape[:lead]
    for s in shapes:
        n = math.prod(s)
        rows = _pack_rows_of(n)
        blk = lax.slice_in_dim(pack, off, off + rows, axis=lead)
        if n % PACK_COLS == 0:
            out.append(lax.slice_in_dim(blk, 0, n // PACK_COLS, axis=lead).reshape(head + tuple(s)))
        else:
            out.append(blk.reshape(head + (rows * PACK_COLS,))[..., :n].reshape(head + tuple(s)))
        off += rows
    return out


def _unpack_gathered(g):
    blocks = _unpack(g, [_shard_shape(s, a) for _, s, a in _SHARDED], lead=1)
    return {name: jnp.moveaxis(blk.astype(f32), 0, axis).reshape(shape)
            for (name, shape, axis), blk in zip(_SHARDED, blocks)}


def _pack_by_dest(grads):
    parts = []
    for name, shape, axis in _SHARDED:
        ss = _shard_shape(shape, axis)
        g = grads[name].reshape(shape[:axis] + (N_DEV, ss[axis]) + shape[axis + 1:])
        parts.append(jnp.moveaxis(g, axis, 0))
    return _pack(parts, lead=1)


_rmsnorm_op = _make_blockop(_f_rmsnorm, "rmsnorm", (D_MODEL, D_MODEL), (640, 256, 128), (bf16, f32))
_swiglu_op = _make_blockop(_f_swiglu, "swiglu", (D_FF,), (256, 128), (bf16,))
_headnorm_op = _make_blockop(_f_headnorm, "sb_headnorm", (GW,), (640, 256, 128))
_dn_prep_op = _make_blockop(_f_dn_prep, "dn_prep", (GW,) * 5, (128,))
_dn_intra_op = _make_blockop(_f_dn_intra_heads, "dn_intra", (HEAD_DIM,) * 6, (256,), whole_lead=True)
_dn_out_op = _make_blockop(_f_dn_out, "dn_out", (GW,), (640, 256, 128))
_s5_param_op = _make_blockop(_f_s5_param, "s5_param", (1, 1, S5_GROUP, S5_GROUP), (256,))
_s5_post_op = _make_blockop(_f_s5_post, "s5_post", (S5_WIDTH,), (256, 128))
_lin_gu, _ = _make_linear("ffn_gu", out_dtype=bf16)
_, _lin_down_res = _make_linear("ffn_down", scale=0.5)
_lin_in, _ = _make_linear("mix_in")
_, _lin_out_res = _make_linear("mix_out")


def _heads(t):
    return jnp.transpose(t.reshape(t.shape[0], N_HEADS, HEAD_DIM), (1, 0, 2))


def _unheads(t):
    return jnp.transpose(t, (1, 0, 2)).reshape(t.shape[1], GW)


def _ffn(h, gnorm, wg, wu, wd):
    xn, h_skip = _rmsnorm_op((h,), (gnorm[None],))
    gu = _lin_gu(xn, jnp.concatenate([wg, wu], axis=1))
    (a,) = _swiglu_op((gu,), ())
    return _lin_down_res(a, wd, h_skip)


def _lane_row(vals, start):
    return jnp.pad(vals, (start, 128 - start - vals.shape[0]))[None]


def _block_diag(t):
    g, a, b = t.shape
    eye = jnp.eye(g, dtype=t.dtype)
    return (t[:, :, None, :] * eye[:, None, :, None]).reshape(g * a, g * b)


def _mixer(h, p):
    xn, h_skip = _rmsnorm_op((h,), (p["mix_norm"][None],))
    w_in = p["w_in"]
    w_pad = jnp.concatenate([w_in[:, :IN_SMALL], jnp.zeros((D_MODEL, C_S5U - IN_SMALL), f32), w_in[:, IN_SMALL:],
                             jnp.zeros((D_MODEL, IN_PAD - C_S5U - S5_WIDTH), f32)], axis=1)
    proj = _lin_in(xn, w_pad)
    o_sb = sb_attention(_heads(proj[:, C_SBQ:C_SBK]), _heads(proj[:, C_SBK:C_SBV]), _heads(proj[:, C_SBV:C_DNQKV]))
    (o_sb,) = _headnorm_op((_unheads(o_sb),), (jnp.tile(p["sb_out_norm"], N_HEADS)[None],))
    conv = conv_op(proj[:, C_DNQKV:C_DNZ], jnp.pad(p["dn_conv_w"], ((0, 8 - DN_CONV), (0, 0))))
    q, k, v, gc, bb = _dn_prep_op((conv, proj[:, C_DNBA:C_S5U]),
                                  (_lane_row(p["dn_a_log"], N_HEADS), _lane_row(p["dn_dt_bias"], N_HEADS)))
    parts = _dn_intra_op(tuple(_heads(t) for t in (q, k, v, gc, bb)), ())
    o_dn = _unheads(dn_scan(*parts))
    (o_dn,) = _dn_out_op((o_dn, proj[:, C_DNZ:C_DNBA]), (jnp.tile(p["dn_out_norm"], N_HEADS)[None],))
    u = proj[:, C_S5U:C_S5U + S5_WIDTH]
    col = lambda t: t.reshape(S5_LANES, 1)
    abr, abi, bbr, bbi = _s5_param_op(
        (col(p["s5_a_re"]), col(p["s5_a_im"]), col(jnp.repeat(p["s5_log_dt"], S5_STATE)),
         p["s5_b_re"].reshape(S5_LANES, S5_GROUP), p["s5_b_im"].reshape(S5_LANES, S5_GROUP)), ())
    to_b = lambda t: _block_diag(jnp.transpose(t.reshape(S5_GROUPS, S5_STATE, S5_GROUP), (0, 2, 1)))
    to_c = lambda t: _block_diag(jnp.transpose(t, (0, 2, 1)))
    y = s5_core(u, jnp.concatenate([to_b(bbr), to_b(bbi)], axis=1),
                jnp.concatenate([to_c(p["s5_c_re"]), -to_c(p["s5_c_im"])], axis=0),
                abr.reshape(1, S5_LANES), abi.reshape(1, S5_LANES))
    (o_s5,) = _s5_post_op((y, u), (p["s5_d"][None], p["s5_w_glu"], p["s5_b_glu"][None], p["s5_out_norm"][None]))
    return _lin_out_res(jnp.concatenate([o_sb, o_dn, o_s5], axis=1), p["w_out"], h_skip)


def _trunk(x2d, w):
    h = jnp.concatenate([jnp.zeros((PAD, D_MODEL), f32), w["meta_tokens"], x2d], axis=0)
    for l in range(DEPTH):
        p = {k: v[l] for k, v in w.items() if k not in ("meta_tokens", "final_norm")}
        h = _ffn(h, p["ffn1_norm"], p["ffn1_w_gate"], p["ffn1_w_up"], p["ffn1_w_down"])
        h = _mixer(h, p)
        h = _ffn(h, p["ffn2_norm"], p["ffn2_w_gate"], p["ffn2_w_up"], p["ffn2_w_down"])
    return h


def _step(x, loss_target, w, m, v):
    s_names = [n for n, _, _ in _SHARDED]
    r_names = [n for n, _ in _REPL]
    s_shapes = [_shard_shape(s, a) for _, s, a in _SHARDED]
    r_shapes = [s for _, s in _REPL] + [(1,)]
    shard_pack = _pack([w[n] for n in s_names])
    gathered = _gather_via_sibling(shard_pack.astype(bf16), "gather_weights")
    full = _unpack_gathered(gathered)
    full.update({n: w[n] for n, _ in _REPL})
    trunk_w = {k: t for k, t in full.items() if k != "final_norm"}
    h, vjp = jax.vjp(_trunk, x[0], trunk_w)
    loss, dh, dgf = _final_call(h, loss_target[0], full["final_norm"][None])
    dx, dw = vjp(dh)
    dw["final_norm"] = dgf[0]
    g_shard = _sum_slots(_exchange(_pack_by_dest(dw).astype(bf16), False, "scatter_grads"), "sum_shard_grads")
    repl_pack = _pack([dw[n] for n in r_names] + [loss.reshape(1)])
    g_repl = _sum_slots(_exchange(repl_pack, True, "gather_small_grads"), "sum_small_grads")
    outs = {"grad_" + n: t for n, t in zip(s_names, _unpack(g_shard, s_shapes))}
    outs.update({"grad_" + n: t for n, t in zip(r_names + ["loss"], _unpack(g_repl, r_shapes))})
    loss_total = outs["grad_loss"][0]
    names = [n for n, _, _ in _WEIGHTS]
    for n in names:
        outs["delta_" + n], outs["new_m_" + n], outs["new_v_" + n] = _adamw(w[n], outs["grad_" + n], m[n], v[n], "adamw_" + n)
    return (loss_total, dx[None], *[outs["grad_" + n] for n in names], *[outs["delta_" + n] for n in names],
            *[outs["new_m_" + n] for n in names], *[outs["new_v_" + n] for n in names])


def kernel(x, meta_tokens, ffn1_norm, ffn1_w_gate, ffn1_w_up, ffn1_w_down, mix_norm, w_in, sb_out_norm, dn_conv_w, dn_a_log, dn_dt_bias, dn_out_norm, s5_a_re, s5_a_im, s5_log_dt, s5_b_re, s5_b_im, s5_c_re, s5_c_im, s5_d, s5_w_glu, s5_b_glu, s5_out_norm, w_out, ffn2_norm, ffn2_w_gate, ffn2_w_up, ffn2_w_down, final_norm, loss_target, m_meta_tokens, m_ffn1_norm, m_ffn1_w_gate, m_ffn1_w_up, m_ffn1_w_down, m_mix_norm, m_w_in, m_sb_out_norm, m_dn_conv_w, m_dn_a_log, m_dn_dt_bias, m_dn_out_norm, m_s5_a_re, m_s5_a_im, m_s5_log_dt, m_s5_b_re, m_s5_b_im, m_s5_c_re, m_s5_c_im, m_s5_d, m_s5_w_glu, m_s5_b_glu, m_s5_out_norm, m_w_out, m_ffn2_norm, m_ffn2_w_gate, m_ffn2_w_up, m_ffn2_w_down, m_final_norm, v_meta_tokens, v_ffn1_norm, v_ffn1_w_gate, v_ffn1_w_up, v_ffn1_w_down, v_mix_norm, v_w_in, v_sb_out_norm, v_dn_conv_w, v_dn_a_log, v_dn_dt_bias, v_dn_out_norm, v_s5_a_re, v_s5_a_im, v_s5_log_dt, v_s5_b_re, v_s5_b_im, v_s5_c_re, v_s5_c_im, v_s5_d, v_s5_w_glu, v_s5_b_glu, v_s5_out_norm, v_w_out, v_ffn2_norm, v_ffn2_w_gate, v_ffn2_w_up, v_ffn2_w_down, v_final_norm):
    names = [n for n, _, _ in _WEIGHTS]
    ws = (meta_tokens, ffn1_norm, ffn1_w_gate, ffn1_w_up, ffn1_w_down, mix_norm, w_in, sb_out_norm, dn_conv_w, dn_a_log, dn_dt_bias, dn_out_norm, s5_a_re, s5_a_im, s5_log_dt, s5_b_re, s5_b_im, s5_c_re, s5_c_im, s5_d, s5_w_glu, s5_b_glu, s5_out_norm, w_out, ffn2_norm, ffn2_w_gate, ffn2_w_up, ffn2_w_down, final_norm)
    ms = (m_meta_tokens, m_ffn1_norm, m_ffn1_w_gate, m_ffn1_w_up, m_ffn1_w_down, m_mix_norm, m_w_in, m_sb_out_norm, m_dn_conv_w, m_dn_a_log, m_dn_dt_bias, m_dn_out_norm, m_s5_a_re, m_s5_a_im, m_s5_log_dt, m_s5_b_re, m_s5_b_im, m_s5_c_re, m_s5_c_im, m_s5_d, m_s5_w_glu, m_s5_b_glu, m_s5_out_norm, m_w_out, m_ffn2_norm, m_ffn2_w_gate, m_ffn2_w_up, m_ffn2_w_down, m_final_norm)
    vs = (v_meta_tokens, v_ffn1_norm, v_ffn1_w_gate, v_ffn1_w_up, v_ffn1_w_down, v_mix_norm, v_w_in, v_sb_out_norm, v_dn_conv_w, v_dn_a_log, v_dn_dt_bias, v_dn_out_norm, v_s5_a_re, v_s5_a_im, v_s5_log_dt, v_s5_b_re, v_s5_b_im, v_s5_c_re, v_s5_c_im, v_s5_d, v_s5_w_glu, v_s5_b_glu, v_s5_out_norm, v_w_out, v_ffn2_norm, v_ffn2_w_gate, v_ffn2_w_up, v_ffn2_w_down, v_final_norm)
    return _step(x, loss_target, dict(zip(names, ws)), dict(zip(names, ms)), dict(zip(names, vs)))
```
